```python
import math
import jax, jax.numpy as jnp
from jax import lax
import numpy as np


D_MODEL = 4096
BATCH = 2
SEQ = 8192
DEPTH = 2

GRID_W = 64
CTX_LEN = 256
F32 = jnp.float32
EPS = 1e-6
ROPE_THETA = 10000.0
Q_BLOCK = 128

N_MOD = 9
D_FF = 2 * D_MODEL

SSD_INNER = D_MODEL // 2
SSD_HEAD_DIM = 64
SSD_HEADS = SSD_INNER // SSD_HEAD_DIM
SSD_GROUPS = 4
SSD_STATE = 128
SSD_CONV = 5
SSD_CHUNK = 128
SSD_GN = SSD_GROUPS * SSD_STATE
SSD_CONV_CH = SSD_INNER + 2 * SSD_GN

MLA_HEADS = 8
MLA_NOPE = 128
MLA_ROPE = 64
MLA_V = 128
MLA_Q_LORA = 1024
MLA_KV_LORA = 512
MLA_QK = MLA_NOPE + MLA_ROPE
MLA_SCALE = 1.0 / math.sqrt(MLA_QK)

DIFF_HEADS = 4
DIFF_QK = 128
DIFF_V = 2 * DIFF_QK
DIFF_SCALE = 1.0 / math.sqrt(DIFF_QK)

SSD_COLS = 2 * SSD_INNER + 2 * SSD_GN + 2 * SSD_HEADS
MLA_COLS = MLA_Q_LORA + MLA_KV_LORA + MLA_ROPE
DIFF_COLS = DIFF_HEADS * (4 * DIFF_QK + DIFF_V)
IN_COLS = SSD_COLS + MLA_COLS + DIFF_COLS
MIX_WIDTH = SSD_INNER + MLA_HEADS * MLA_V + DIFF_HEADS * DIFF_V

kernel_name = "hybrid_ssd_mla_diffattn_macaron_dit"


def rmsnorm(x, g):
    xf = x.astype(F32)
    y = xf * lax.rsqrt(jnp.mean(xf * xf, -1, keepdims=True) + EPS)
    return (y * g.astype(F32)).astype(x.dtype)


def swiglu(h, w_gate, w_up, w_down):
    return (jax.nn.silu(h @ w_gate) * (h @ w_up)) @ w_down


def axial_tables(rows, rot_dim):
    row = jnp.repeat(jnp.arange(rows, dtype=F32), GRID_W)
    col = (jnp.arange(rows * GRID_W) % GRID_W).astype(F32)
    axis_dim = rot_dim // 2
    inv = ROPE_THETA ** (-jnp.arange(0, axis_dim, 2, dtype=F32) / axis_dim)
    ang = jnp.stack([row[:, None] * inv, col[:, None] * inv], axis=1)
    return jnp.cos(ang), jnp.sin(ang)


def rope_2d(x, cos, sin):
    b, l, h, r = x.shape
    xa = x.reshape(b, l, h, 2, r // 2).astype(F32)
    half = r // 4
    x1, x2 = xa[..., :half], xa[..., half:]
    cs, sn = cos[:, None], sin[:, None]
    out = jnp.concatenate([x1 * cs - x2 * sn, x2 * cs + x1 * sn], -1)
    return out.reshape(b, l, h, r).astype(x.dtype)


def dwconv_centred(x, w, bias):
    k = w.shape[0]
    y = lax.conv_general_dilated(
        x, w[:, None, :].astype(x.dtype), window_strides=(1,),
        padding=((k // 2, k // 2),), dimension_numbers=('NWC', 'WIO', 'NWC'),
        feature_group_count=x.shape[-1])
    return y + bias.astype(x.dtype)


def segsum_exp(a):
    cs = jnp.cumsum(a, -1)
    t = a.shape[-1]
    mask = jnp.tril(jnp.ones((t, t), bool))
    diff = cs[..., :, None] - cs[..., None, :]
    return jnp.where(mask, jnp.exp(jnp.where(mask, diff, 0.0)), 0.0)


def ssd_scan(xh, dt, a, bm, cm, h0):
    b, l, h, p = xh.shape
    g, n = bm.shape[2], bm.shape[3]
    j = h // g
    q = SSD_CHUNK
    nc = l // q
    X = (xh * dt[..., None]).reshape(b, nc, q, g, j, p)
    A = (dt * a).reshape(b, nc, q, g, j).transpose(0, 3, 4, 1, 2)
    Bc = bm.reshape(b, nc, q, g, n)
    Cc = cm.reshape(b, nc, q, g, n)
    A_cs = jnp.cumsum(A, -1)
    Lmat = segsum_exp(A)
    y_diag = jnp.einsum('bclgn,bcsgn,bgjcls,bcsgjp->bclgjp', Cc, Bc, Lmat, X)
    decay_states = jnp.exp(A_cs[..., -1:] - A_cs)
    states = jnp.einsum('bclgn,bgjcl,bclgjp->bcgjpn', Bc, decay_states, X)
    states = jnp.concatenate([h0.reshape(b, 1, g, j, p, n), states], 1)
    chunk_decay = segsum_exp(jnp.pad(A_cs[..., -1], ((0, 0), (0, 0), (0, 0), (1, 0))))
    states = jnp.einsum('bgjzc,bcgjpn->bzgjpn', chunk_decay, states)
    prev, final = states[:, :-1], states[:, -1]
    y_off = jnp.einsum('bclgn,bcgjpn,bgjcl->bclgjp', Cc, prev, jnp.exp(A_cs))
    y = (y_diag + y_off).reshape(b, l, h, p)
    return y, final.reshape(b, h, p, n)


def ssd_branch(proj, p, h0_f, h0_b):
    b, l, _ = proj.shape
    z = proj[..., :SSD_INNER]
    xbc = proj[..., SSD_INNER:SSD_INNER + SSD_CONV_CH]
    dt_raw = proj[..., SSD_INNER + SSD_CONV_CH:].astype(F32).reshape(b, l, 2, SSD_HEADS)
    xbc = jax.nn.silu(dwconv_centred(xbc, p['ssd_conv_w'], p['ssd_conv_b'])).astype(F32)
    xs = xbc[..., :SSD_INNER].reshape(b, l, SSD_HEADS, SSD_HEAD_DIM)
    bm = xbc[..., SSD_INNER:SSD_INNER + SSD_GN].reshape(b, l, SSD_GROUPS, SSD_STATE)
    cm = xbc[..., SSD_INNER + SSD_GN:].reshape(b, l, SSD_GROUPS, SSD_STATE)
    dt = jax.nn.softplus(dt_raw + p['ssd_dt_bias'].astype(F32))
    a = -jnp.exp(p['ssd_a_log'].astype(F32))
    y_f, h_f = ssd_scan(xs, dt[:, :, 0], a[0], bm, cm, h0_f)
    y_b, h_b = ssd_scan(xs[:, ::-1], dt[:, ::-1, 1], a[1], bm[:, ::-1], cm[:, ::-1], h0_b)
    y = y_f + y_b[:, ::-1] + p['ssd_d'].astype(F32)[:, None] * xs
    y = y.reshape(b, l, SSD_INNER) * jax.nn.silu(z.astype(F32))
    return rmsnorm(y, p['ssd_norm_g']).astype(proj.dtype), h_f, h_b


def mla_qkv(proj, p, rope):
    b, l, _ = proj.shape
    cq = proj[..., :MLA_Q_LORA]
    ckv = proj[..., MLA_Q_LORA:MLA_Q_LORA + MLA_KV_LORA]
    k_rope = proj[..., MLA_Q_LORA + MLA_KV_LORA:]
    q = (rmsnorm(cq, p['mla_g_cq']) @ p['mla_w_uq']).reshape(b, l, MLA_HEADS, MLA_QK)
    kv = (rmsnorm(ckv, p['mla_g_ckv']) @ p['mla_w_ukv']).reshape(b, l, MLA_HEADS, MLA_NOPE + MLA_V)
    k = jnp.concatenate([kv[..., :MLA_NOPE],
                         jnp.broadcast_to(k_rope[:, :, None, :], (b, l, MLA_HEADS, MLA_ROPE))], -1)
    v = kv[..., MLA_NOPE:]
    q = rmsnorm(q, p['mla_g_q'])
    k = rmsnorm(k, p['mla_g_k'])
    if rope is not None:
        cos, sin = rope
        q = jnp.concatenate([q[..., :MLA_NOPE], rope_2d(q[..., MLA_NOPE:], cos, sin)], -1)
        k = jnp.concatenate([k[..., :MLA_NOPE], rope_2d(k[..., MLA_NOPE:], cos, sin)], -1)
    return q, k, v


def diff_qkv(proj, p, rope):
    b, l, _ = proj.shape
    w = DIFF_HEADS * 2 * DIFF_QK
    q = rmsnorm(proj[..., :w].reshape(b, l, 2 * DIFF_HEADS, DIFF_QK), p['diff_g_q'])
    k = rmsnorm(proj[..., w:2 * w].reshape(b, l, 2 * DIFF_HEADS, DIFF_QK), p['diff_g_k'])
    v = proj[..., 2 * w:].reshape(b, l, DIFF_HEADS, DIFF_V)
    if rope is not None:
        cos, sin = rope
        q = rope_2d(q, cos, sin)
        k = rope_2d(k, cos, sin)
    return (q.reshape(b, l, DIFF_HEADS, 2, DIFF_QK), k.reshape(b, l, DIFF_HEADS, 2, DIFF_QK), v)


def softmax_block(qb, k, v, scale):
    s = jnp.einsum('bqhd,bkhd->bhqk', qb, k).astype(F32) * scale
    pr = jax.nn.softmax(s, -1).astype(v.dtype)
    return jnp.einsum('bhqk,bkhd->bqhd', pr, v)


def diff_block(qb, k, v, lam, scale):
    s = jnp.einsum('bqhcd,bkhcd->bhcqk', qb, k).astype(F32) * scale
    pr = jax.nn.softmax(s, -1)
    w = (pr[:, :, 0] - lam * pr[:, :, 1]).astype(v.dtype)
    return jnp.einsum('bhqk,bkhd->bqhd', w, v)


def sweep_query_blocks(fn, q):
    b, l = q.shape[0], q.shape[1]
    n = l // Q_BLOCK
    qb = jnp.moveaxis(q.reshape((b, n, Q_BLOCK) + q.shape[2:]), 1, 0)
    out = jnp.moveaxis(lax.map(fn, qb), 0, 1)
    return out.reshape((b, l) + out.shape[3:])


def token_mixing(u, uc, p, rope_mla, rope_diff, layer_idx, need_ctx):
    b = u.shape[0]
    proj = u @ p['w_in']
    proj_c = uc @ p['w_in']
    s1, s2 = SSD_COLS, SSD_COLS + MLA_COLS
    h0 = jnp.zeros((b, SSD_HEADS, SSD_HEAD_DIM, SSD_STATE), F32)
    ssd_c, hf_c, hb_c = ssd_branch(proj_c[..., :s1], p, h0, h0)
    ssd_l, _, _ = ssd_branch(proj[..., :s1], p, hf_c, hb_c)
    q_c, k_c, v_c = mla_qkv(proj_c[..., s1:s2], p, None)
    q, k, v = mla_qkv(proj[..., s1:s2], p, rope_mla)
    k_all = jnp.concatenate([k_c, k], 1)
    v_all = jnp.concatenate([v_c, v], 1)
    mla_l = sweep_query_blocks(lambda qb: softmax_block(qb, k_all, v_all, MLA_SCALE), q)
    lam_init = 0.8 - 0.6 * math.exp(-0.3 * layer_idx)
    lv = p['diff_lambda'].astype(F32)
    lam = jnp.exp(jnp.sum(lv[0] * lv[1])) - jnp.exp(jnp.sum(lv[2] * lv[3])) + lam_init
    dq_c, dk_c, dv_c = diff_qkv(proj_c[..., s2:], p, None)
    dq, dk, dv = diff_qkv(proj[..., s2:], p, rope_diff)
    dk_all = jnp.concatenate([dk_c, dk], 1)
    dv_all = jnp.concatenate([dv_c, dv], 1)
    diff_l = sweep_query_blocks(lambda qb: diff_block(qb, dk_all, dv_all, lam, DIFF_SCALE), dq)

    def merge(ssd_o, mla_o, diff_o):
        n, t = ssd_o.shape[0], ssd_o.shape[1]
        diff_o = rmsnorm(diff_o, p['diff_subln_g']) * (1.0 - lam_init)
        cat = jnp.concatenate([ssd_o, mla_o.reshape(n, t, -1), diff_o.reshape(n, t, -1)], -1)
        return cat @ p['w_out']

    y = merge(ssd_l, mla_l, diff_l)
    if not need_ctx:
        return y, None
    mla_c = softmax_block(q_c, k_c, v_c, MLA_SCALE)
    diff_c = diff_block(dq_c, dk_c, dv_c, lam, DIFF_SCALE)
    return y, merge(ssd_c, mla_c, diff_c)


def trunk_layer(x, ctx, c, c_ctx, p, rope_mla, rope_diff, layer_idx, need_ctx):
    b = x.shape[0]
    mod = (jax.nn.silu(c) @ p['w_mod'] + p['b_mod']).reshape(b, N_MOD, 1, D_MODEL)
    mod_c = (jax.nn.silu(c_ctx) @ p['w_mod'] + p['b_mod']).reshape(1, N_MOD, 1, D_MODEL)
    g = p['g_norm']

    def pre(h, m, i):
        return rmsnorm(h, g[i]) * (1.0 + m[:, 3 * i + 1]) + m[:, 3 * i]

    def ffn(h, j):
        return swiglu(h, p['w_ffn_gate'][j], p['w_ffn_up'][j], p['w_ffn_down'][j])

    x = x + 0.5 * mod[:, 2] * ffn(pre(x, mod, 0), 0)
    ctx = ctx + 0.5 * mod_c[:, 2] * ffn(pre(ctx, mod_c, 0), 0)
    y, y_c = token_mixing(pre(x, mod, 1), pre(ctx, mod_c, 1), p, rope_mla, rope_diff,
                          layer_idx, need_ctx)
    x = x + mod[:, 5] * y
    x = x + 0.5 * mod[:, 8] * ffn(pre(x, mod, 2), 1)
    if need_ctx:
        ctx = ctx + mod_c[:, 5] * y_c
        ctx = ctx + 0.5 * mod_c[:, 8] * ffn(pre(ctx, mod_c, 2), 1)
    return x, ctx


def setup_inputs(seed: int = 0) -> dict:
    key = jax.random.key(seed)
    ks = jax.random.split(key, 40)
    keys = iter([ks[i] for i in range(40)])
    D = D_MODEL

    def nrm(shape, scale):
        return jax.random.normal(next(keys), shape, F32) * scale

    def gain(shape):
        return 1.0 + nrm(shape, 0.02)

    dt0 = jnp.exp(jax.random.uniform(next(keys), (DEPTH, 2, SSD_HEADS), F32,
                                     math.log(1e-3), math.log(1e-1)))
    return {
        'x': nrm((BATCH, SEQ, D), 1.0),
        'c': nrm((BATCH, D), 1.0),
        'ctx': nrm((BATCH, CTX_LEN, D), 1.0),
        'c_ctx': nrm((D,), 1.0),
        'w_mod': nrm((DEPTH, D, N_MOD * D), 0.5 * D ** -0.5),
        'b_mod': nrm((DEPTH, N_MOD * D), 0.01),
        'g_norm': gain((DEPTH, 3, D)),
        'w_ffn_gate': nrm((DEPTH, 2, D, D_FF), D ** -0.5),
        'w_ffn_up': nrm((DEPTH, 2, D, D_FF), D ** -0.5),
        'w_ffn_down': nrm((DEPTH, 2, D_FF, D), D_FF ** -0.5),
        'w_in': nrm((DEPTH, D, IN_COLS), D ** -0.5),
        'ssd_conv_w': nrm((DEPTH, SSD_CONV, SSD_CONV_CH), SSD_CONV ** -0.5),
        'ssd_conv_b': nrm((DEPTH, SSD_CONV_CH), 0.01),
        'ssd_dt_bias': dt0 + jnp.log(-jnp.expm1(-dt0)),
        'ssd_a_log': jnp.log(jax.random.uniform(next(keys), (DEPTH, 2, SSD_HEADS), F32, 1.0, 16.0)),
        'ssd_d': gain((DEPTH, SSD_HEADS)),
        'ssd_norm_g': gain((DEPTH, SSD_INNER)),
        'mla_g_cq': gain((DEPTH, MLA_Q_LORA)),
        'mla_w_uq': nrm((DEPTH, MLA_Q_LORA, MLA_HEADS * MLA_QK), MLA_Q_LORA ** -0.5),
        'mla_g_ckv': gain((DEPTH, MLA_KV_LORA)),
        'mla_w_ukv': nrm((DEPTH, MLA_KV_LORA, MLA_HEADS * (MLA_NOPE + MLA_V)), MLA_KV_LORA ** -0.5),
        'mla_g_q': gain((DEPTH, MLA_QK)),
        'mla_g_k': gain((DEPTH, MLA_QK)),
        'diff_g_q': gain((DEPTH, DIFF_QK)),
        'diff_g_k': gain((DEPTH, DIFF_QK)),
        'diff_lambda': nrm((DEPTH, 4, DIFF_QK), 0.1),
        'diff_subln_g': gain((DEPTH, DIFF_V)),
        'w_out': nrm((DEPTH, MIX_WIDTH, D), MIX_WIDTH ** -0.5),
    }


def reference(x, c, ctx, c_ctx, w_mod, b_mod, g_norm, w_ffn_gate, w_ffn_up, w_ffn_down,
              w_in, ssd_conv_w, ssd_conv_b, ssd_dt_bias, ssd_a_log, ssd_d, ssd_norm_g,
              mla_g_cq, mla_w_uq, mla_g_ckv, mla_w_ukv, mla_g_q, mla_g_k,
              diff_g_q, diff_g_k, diff_lambda, diff_subln_g, w_out):
    n_tok = x.shape[1]
    rows = n_tok // GRID_W
    rope_mla = axial_tables(rows, MLA_ROPE)
    rope_diff = axial_tables(rows, DIFF_QK)
    for li in range(DEPTH):
        p = {
            'w_mod': w_mod[li], 'b_mod': b_mod[li], 'g_norm': g_norm[li],
            'w_ffn_gate': w_ffn_gate[li], 'w_ffn_up': w_ffn_up[li], 'w_ffn_down': w_ffn_down[li],
            'w_in': w_in[li], 'ssd_conv_w': ssd_conv_w[li], 'ssd_conv_b': ssd_conv_b[li],
            'ssd_dt_bias': ssd_dt_bias[li], 'ssd_a_log': ssd_a_log[li], 'ssd_d': ssd_d[li],
            'ssd_norm_g': ssd_norm_g[li], 'mla_g_cq': mla_g_cq[li], 'mla_w_uq': mla_w_uq[li],
            'mla_g_ckv': mla_g_ckv[li], 'mla_w_ukv': mla_w_ukv[li], 'mla_g_q': mla_g_q[li],
            'mla_g_k': mla_g_k[li], 'diff_g_q': diff_g_q[li], 'diff_g_k': diff_g_k[li],
            'diff_lambda': diff_lambda[li], 'diff_subln_g': diff_subln_g[li], 'w_out': w_out[li],
        }
        x, ctx = trunk_layer(x, ctx, c, c_ctx, p, rope_mla, rope_diff, li, li < DEPTH - 1)
    return x
```

```python
import functools
import math

import jax
import jax.numpy as jnp
from jax import lax
from jax.experimental import pallas as pl
from jax.experimental.pallas import tpu as pltpu

F32 = jnp.float32
BF16 = jnp.bfloat16

GRID_W = 64
EPS = 1e-6
ROPE_THETA = 10000.0
N_MOD = 9
SSD_INNER = 2048
SSD_HEAD_DIM = 64
SSD_HEADS = 32
SSD_GROUPS = 4
SSD_STATE = 128
SSD_CONV = 5
SSD_CHUNK = 128
SSD_GN = SSD_GROUPS * SSD_STATE
SSD_CONV_CH = SSD_INNER + 2 * SSD_GN
SSD_COLS = 2 * SSD_INNER + 2 * SSD_GN + 2 * SSD_HEADS
MLA_HEADS = 8
MLA_NOPE = 128
MLA_ROPE = 64
MLA_V = 128
MLA_Q_LORA = 1024
MLA_KV_LORA = 512
MLA_QK = MLA_NOPE + MLA_ROPE
MLA_QK_PAD = 256
MLA_SCALE = 1.0 / math.sqrt(MLA_QK)
MLA_COLS = MLA_Q_LORA + MLA_KV_LORA + MLA_ROPE
DIFF_HEADS = 4
DIFF_QK = 128
DIFF_V = 2 * DIFF_QK
DIFF_SCALE = 1.0 / math.sqrt(DIFF_QK)
DIFF_COLS = DIFF_HEADS * (4 * DIFF_QK + DIFF_V)
IN_COLS = SSD_COLS + MLA_COLS + DIFF_COLS

LANES = 128
BF16_SUBLANES = 16
MOD_ROWS = 8
VMEM_LIMIT = 50 * 1024 * 1024

PC_Z = 0
PC_XS = 2048
PC_DQ = 4096
PC_DK = 5120
PC_DV = 6144
PC_CQ = 7168
PC_B = 8192
PC_C = 8704
PC_CKV = 9216
PC_KR = 9728
PC_PAD = 9856
PROJ_COLS = 9984


def _cparams(sem):
    return pltpu.CompilerParams(dimension_semantics=sem, vmem_limit_bytes=VMEM_LIMIT)


def _pick(n, prefs):
    for p in prefs:
        if n % p == 0:
            return p
    return n


def _silu(x):
    return x / (1.0 + jnp.exp(-x))


def _mm_kernel(a_ref, b_ref, o_ref):
    o_ref[...] = jnp.dot(a_ref[...], b_ref[...], preferred_element_type=F32).astype(o_ref.dtype)


def _matmul(a, b, m_rows, out_dtype, tm, tn):
    k = a.shape[1]
    n = b.shape[1]
    return pl.pallas_call(
        _mm_kernel,
        grid=(n // tn, m_rows // tm),
        in_specs=[pl.BlockSpec((tm, k), lambda j, i: (i, 0)),
                  pl.BlockSpec((k, tn), lambda j, i: (0, j))],
        out_specs=pl.BlockSpec((tm, tn), lambda j, i: (i, j)),
        out_shape=jax.ShapeDtypeStruct((m_rows, n), out_dtype),
        compiler_params=_cparams(("parallel", "parallel")),
        name="mm_plain",
    )(a, b)


def _swiglu_kernel(a_ref, wg_ref, wu_ref, o_ref):
    a = a_ref[...]
    g = jnp.dot(a, wg_ref[...], preferred_element_type=F32)
    u = jnp.dot(a, wu_ref[...], preferred_element_type=F32)
    o_ref[...] = (_silu(g) * u).astype(o_ref.dtype)


def _matmul_swiglu(a, wg, wu, m_rows, tm, tn):
    k = a.shape[1]
    n = wg.shape[1]
    return pl.pallas_call(
        _swiglu_kernel,
        grid=(n // tn, m_rows // tm),
        in_specs=[pl.BlockSpec((tm, k), lambda j, i: (i, 0)),
                  pl.BlockSpec((k, tn), lambda j, i: (0, j)),
                  pl.BlockSpec((k, tn), lambda j, i: (0, j))],
        out_specs=pl.BlockSpec((tm, tn), lambda j, i: (i, j)),
        out_shape=jax.ShapeDtypeStruct((m_rows, n), BF16),
        compiler_params=_cparams(("parallel", "parallel")),
        name="mm_swiglu",
    )(a, wg, wu)


def _resid_kernel(*refs, n_pairs, scale):
    x_ref, gate_ref, o_ref = refs[2 * n_pairs], refs[2 * n_pairs + 1], refs[2 * n_pairs + 2]
    acc = jnp.dot(refs[0][...], refs[1][...], preferred_element_type=F32)
    for p in range(1, n_pairs):
        acc = acc + jnp.dot(refs[2 * p][...], refs[2 * p + 1][...], preferred_element_type=F32)
    o_ref[...] = x_ref[...] + (scale * gate_ref[...]) * acc


def _matmul_resid(pairs, x, mod4, gate_idx, scale, m_rows, group_of_row, tm, tn):
    d = x.shape[1]
    in_specs = []
    args = []
    for a, b in pairs:
        in_specs.append(pl.BlockSpec((tm, a.shape[1]), lambda j, i: (i, 0)))
        in_specs.append(pl.BlockSpec((b.shape[0], tn), lambda j, i: (0, j)))
        args += [a, b]
    in_specs.append(pl.BlockSpec((tm, tn), lambda j, i: (i, j)))
    gate_blk = gate_idx * (d // tn)
    in_specs.append(pl.BlockSpec((None, 1, tn), lambda j, i: (group_of_row(i * tm), 0, gate_blk + j)))
    args += [x, mod4]
    return pl.pallas_call(
        functools.partial(_resid_kernel, n_pairs=len(pairs), scale=scale),
        grid=(d // tn, m_rows // tm),
        in_specs=in_specs,
        out_specs=pl.BlockSpec((tm, tn), lambda j, i: (i, j)),
        out_shape=jax.ShapeDtypeStruct((m_rows, d), F32),
        compiler_params=_cparams(("parallel", "parallel")),
        name="mm_resid",
    )(*args)


def _mod_kernel(c_ref, w_ref, b_ref, o_ref):
    s = _silu(c_ref[...]).astype(BF16)
    w = w_ref[...].astype(BF16)
    o_ref[...] = jnp.dot(s, w, preferred_element_type=F32) + b_ref[...]


def _modulation(c_rows, w_mod, b_mod):
    depth, d, n = w_mod.shape
    tn = _pick(n, (512, 256, 128))
    return pl.pallas_call(
        _mod_kernel,
        grid=(depth, n // tn),
        in_specs=[pl.BlockSpec((MOD_ROWS, d), lambda l, j: (0, 0)),
                  pl.BlockSpec((None, d, tn), lambda l, j: (l, 0, j)),
                  pl.BlockSpec((None, 1, tn), lambda l, j: (l, 0, j))],
        out_specs=pl.BlockSpec((None, MOD_ROWS, tn), lambda l, j: (l, 0, j)),
        out_shape=jax.ShapeDtypeStruct((depth, MOD_ROWS, n), F32),
        compiler_params=_cparams(("parallel", "parallel")),
        name="modulation",
    )(c_rows, w_mod, b_mod.reshape(depth, 1, n))


def _norm_mod_kernel(x_ref, g_ref, shift_ref, scale_ref, o_ref):
    x = x_ref[...]
    y = x * lax.rsqrt(jnp.mean(x * x, axis=-1, keepdims=True) + EPS) * g_ref[...]
    o_ref[...] = (y * (1.0 + scale_ref[...]) + shift_ref[...]).astype(o_ref.dtype)


def _norm_mod(x, g, mod4, idx, m_rows, group_of_row, tm):
    d = x.shape[1]
    return pl.pallas_call(
        _norm_mod_kernel,
        grid=(m_rows // tm,),
        in_specs=[pl.BlockSpec((tm, d), lambda i: (i, 0)),
                  pl.BlockSpec((1, d), lambda i: (0, 0)),
                  pl.BlockSpec((None, 1, d), lambda i: (group_of_row(i * tm), 0, 3 * idx)),
                  pl.BlockSpec((None, 1, d), lambda i: (group_of_row(i * tm), 0, 3 * idx + 1))],
        out_specs=pl.BlockSpec((tm, d), lambda i: (i, 0)),
        out_shape=jax.ShapeDtypeStruct((m_rows, d), BF16),
        compiler_params=_cparams(("parallel",)),
        name="norm_mod",
    )(x, g.reshape(1, d), mod4, mod4)


def _conv_kernel(prev_ref, cur_ref, next_ref, w_ref, b_ref, o_ref, ext_ref, *, tr, lat_rows, seq, ctx_len):
    r0 = pl.program_id(0) * tr
    in_lat = r0 < lat_rows
    off = jnp.where(in_lat, r0 % seq, (r0 - lat_rows) % ctx_len)
    seg = jnp.where(in_lat, seq, ctx_len)
    first = off == 0
    last = off + tr == seg
    h = BF16_SUBLANES
    ext_ref[0:h, :] = jnp.where(first, 0.0, prev_ref[...].astype(F32))
    ext_ref[h:h + tr, :] = cur_ref[...].astype(F32)
    ext_ref[h + tr:2 * h + tr, :] = jnp.where(last, 0.0, next_ref[...].astype(F32))
    acc = b_ref[...] + w_ref[0:1, :] * ext_ref[pl.ds(h - 2, tr), :]
    for k in range(1, SSD_CONV):
        acc = acc + w_ref[k:k + 1, :] * ext_ref[pl.ds(h - 2 + k, tr), :]
    o_ref[...] = _silu(acc).astype(o_ref.dtype)


def _ssd_conv(proj, conv_w, conv_b, n_rows, lat_rows, seq, ctx_len, tr):
    tc = 512
    h = BF16_SUBLANES
    rb = tr // h
    n_halo = n_rows // h
    xs_blk, bc_blk = PC_XS // tc, PC_B // tc
    n_xs = SSD_INNER // tc

    def col(j):
        return jnp.where(j < n_xs, xs_blk + j, bc_blk + (j - n_xs))

    return pl.pallas_call(
        functools.partial(_conv_kernel, tr=tr, lat_rows=lat_rows, seq=seq, ctx_len=ctx_len),
        grid=(n_rows // tr, SSD_CONV_CH // tc),
        in_specs=[pl.BlockSpec((h, tc), lambda i, j: (jnp.maximum(i * rb - 1, 0), col(j))),
                  pl.BlockSpec((tr, tc), lambda i, j: (i, col(j))),
                  pl.BlockSpec((h, tc), lambda i, j: (jnp.minimum((i + 1) * rb, n_halo - 1), col(j))),
                  pl.BlockSpec((SSD_CONV, tc), lambda i, j: (0, j)),
                  pl.BlockSpec((1, tc), lambda i, j: (0, j))],
        out_specs=pl.BlockSpec((tr, tc), lambda i, j: (i, j)),
        out_shape=jax.ShapeDtypeStruct((n_rows, SSD_CONV_CH), BF16),
        scratch_shapes=[pltpu.VMEM((tr + 2 * h, tc), F32)],
        compiler_params=_cparams(("parallel", "parallel")),
        name="ssd_conv",
    )(proj, proj, proj, conv_w, conv_b.reshape(1, SSD_CONV_CH))


N_PAIRS = SSD_HEADS // 2
PAIRS_PER_GROUP = N_PAIRS // SSD_GROUPS


def _ssd_direction(d, x_ref, b_ref, c_ref, dt_ref, bias_ref, alog_ref, y_ref, st_ref,
                   cs_ref, din_ref, wout_ref, cst_ref, dtt_ref, g_ref, bt_ref, yoff_ref, xp_ref, yp_ref):
    q = SSD_CHUNK
    lane = lax.broadcasted_iota(jnp.int32, (q, LANES), 1)
    row = lax.broadcasted_iota(jnp.int32, (q, q), 0)
    col = lax.broadcasted_iota(jnp.int32, (q, q), 1)
    mask = (row >= col) if d == 0 else (row <= col)

    z = dt_ref[...] + bias_ref[...]
    dt = jnp.maximum(z, 0.0) + jnp.log1p(jnp.exp(-jnp.abs(z)))
    a_dt = dt * (-jnp.exp(alog_ref[...]))
    hi = a_dt.astype(BF16)
    r1 = a_dt - hi.astype(F32)
    mid = r1.astype(BF16)
    lo = (r1 - mid.astype(F32)).astype(BF16)
    tri = jnp.where(mask, 1.0, 0.0).astype(BF16)
    c3 = jnp.dot(tri, jnp.concatenate([hi, mid, lo], axis=1), preferred_element_type=F32)
    cs = c3[:, 0:LANES] + c3[:, LANES:2 * LANES] + c3[:, 2 * LANES:3 * LANES]
    end = q - 1 if d == 0 else 0
    a_tot = cs[end:end + 1, :]
    cs_ref[...] = cs
    din_ref[...] = jnp.exp(cs)
    wout_ref[...] = dt * jnp.exp(a_tot - cs)
    cst_ref[...] = cs.T
    dtt_ref[...] = dt.T

    for g in range(SSD_GROUPS):
        bg = b_ref[:, g * SSD_STATE:(g + 1) * SSD_STATE]
        cg = c_ref[:, g * SSD_STATE:(g + 1) * SSD_STATE]
        g_ref[g] = lax.dot_general(cg, bg, (((1,), (1,)), ((), ())), preferred_element_type=F32)
        bt_ref[g] = bg.astype(F32).T.astype(BF16)
        for pp in range(PAIRS_PER_GROUP):
            p = g * PAIRS_PER_GROUP + pp
            yoff_ref[p] = jnp.dot(cg, st_ref[p].astype(BF16), preferred_element_type=F32)
    for p in range(N_PAIRS):
        xp_ref[p] = x_ref[:, p * LANES:(p + 1) * LANES]

    half = lane < SSD_HEAD_DIM

    def column(ref, h):
        return jnp.sum(jnp.where(lane == h, ref[...], 0.0), axis=1, keepdims=True)

    def pair_body(p, carry):
        g = p // PAIRS_PER_GROUP
        h0 = 2 * p + SSD_HEADS * d
        gmat = g_ref[g]
        xp = xp_ref[p]
        ys = []
        for hh in (h0, h0 + 1):
            diff = column(cs_ref, hh) - cst_ref[pl.ds(hh, 1), :]
            lmat = jnp.exp(jnp.where(mask, diff, -1e30))
            m = (gmat * lmat * dtt_ref[pl.ds(hh, 1), :]).astype(BF16)
            ys.append(jnp.dot(m, xp, preferred_element_type=F32))
        dsc = jnp.where(half, column(din_ref, h0), column(din_ref, h0 + 1))
        wsc = jnp.where(half, column(wout_ref, h0), column(wout_ref, h0 + 1))
        yp_ref[p] = jnp.where(half, ys[0], ys[1]) + yoff_ref[p] * dsc
        xd = (xp.astype(F32) * wsc).astype(BF16)
        upd = jnp.dot(bt_ref[g], xd, preferred_element_type=F32)
        st_ref[p] = st_ref[p] * dsc[end:end + 1, :] + upd
        return carry

    lax.fori_loop(0, N_PAIRS, pair_body, 0)
    for p in range(N_PAIRS):
        y_ref[:, p * LANES:(p + 1) * LANES] = yp_ref[p]


def _ssd_kernel(xf_ref, bf_ref, cf_ref, dtf_ref, xb_ref, bb_ref, cb_ref, dtb_ref, bias_ref, alog_ref,
                yf_ref, yb_ref, stf_ref, stb_ref, cs_ref, din_ref, wout_ref, cst_ref, dtt_ref,
                g_ref, bt_ref, yoff_ref, xp_ref, yp_ref):
    @pl.when(pl.program_id(1) == 0)
    def _():
        stf_ref[...] = jnp.zeros_like(stf_ref)
        stb_ref[...] = jnp.zeros_like(stb_ref)

    shared = (cs_ref, din_ref, wout_ref, cst_ref, dtt_ref, g_ref, bt_ref, yoff_ref, xp_ref, yp_ref)
    _ssd_direction(0, xf_ref, bf_ref, cf_ref, dtf_ref, bias_ref, alog_ref, yf_ref, stf_ref, *shared)
    _ssd_direction(1, xb_ref, bb_ref, cb_ref, dtb_ref, bias_ref, alog_ref, yb_ref, stb_ref, *shared)


def _ssd_scan(xbc, dt_raw, dt_bias, a_log, batch, seq, ctx_len):
    q = SSD_CHUNK
    n_rows = xbc.shape[0]
    ncc, nlc = ctx_len // q, seq // q
    lat_blocks = batch * seq // q

    def fwd_blk(b, s):
        return jnp.where(s < ncc, lat_blocks + b * ncc + s, b * nlc + (s - ncc))

    def bwd_blk(b, s):
        return jnp.where(s < ncc, lat_blocks + b * ncc + (ncc - 1 - s), b * nlc + (nlc - 1 - (s - ncc)))

    def specs(blk):
        return [pl.BlockSpec((q, SSD_INNER), lambda b, s: (blk(b, s), 0)),
                pl.BlockSpec((q, SSD_GN), lambda b, s: (blk(b, s), SSD_INNER // SSD_GN)),
                pl.BlockSpec((q, SSD_GN), lambda b, s: (blk(b, s), SSD_INNER // SSD_GN + 1)),
                pl.BlockSpec((q, LANES), lambda b, s: (blk(b, s), 0))]

    pad = jnp.zeros((LANES - 2 * SSD_HEADS,), F32)
    bias = jnp.concatenate([dt_bias.reshape(-1), pad]).reshape(1, LANES)
    alog = jnp.concatenate([a_log.reshape(-1), pad]).reshape(1, LANES)
    tile = (q, LANES)
    return pl.pallas_call(
        _ssd_kernel,
        grid=(batch, ncc + nlc),
        in_specs=specs(fwd_blk) + specs(bwd_blk) + [pl.BlockSpec((1, LANES), lambda b, s: (0, 0))] * 2,
        out_specs=[pl.BlockSpec((q, SSD_INNER), lambda b, s: (fwd_blk(b, s), 0)),
                   pl.BlockSpec((q, SSD_INNER), lambda b, s: (bwd_blk(b, s), 0))],
        out_shape=[jax.ShapeDtypeStruct((n_rows, SSD_INNER), F32)] * 2,
        scratch_shapes=[pltpu.VMEM((N_PAIRS, SSD_STATE, LANES), F32),
                        pltpu.VMEM((N_PAIRS, SSD_STATE, LANES), F32),
                        pltpu.VMEM(tile, F32), pltpu.VMEM(tile, F32), pltpu.VMEM(tile, F32),
                        pltpu.VMEM(tile, F32), pltpu.VMEM(tile, F32),
                        pltpu.VMEM((SSD_GROUPS, q, q), F32),
                        pltpu.VMEM((SSD_GROUPS, SSD_STATE, q), BF16),
                        pltpu.VMEM((N_PAIRS, q, LANES), F32),
                        pltpu.VMEM((N_PAIRS, q, LANES), BF16),
                        pltpu.VMEM((N_PAIRS, q, LANES), F32)],
        compiler_params=_cparams(("parallel", "arbitrary")),
        name="ssd_scan",
    )(xbc, xbc, xbc, dt_raw, xbc, xbc, xbc, dt_raw, bias, alog)


def _ssd_out_kernel(yf_ref, yb_ref, x_ref, z_ref, d_ref, g_ref, o_ref):
    z = z_ref[...].astype(F32)
    y = (yf_ref[...] + yb_ref[...] + d_ref[...] * x_ref[...].astype(F32)) * _silu(z)
    y = y * lax.rsqrt(jnp.mean(y * y, axis=-1, keepdims=True) + EPS) * g_ref[...]
    o_ref[...] = y.astype(o_ref.dtype)


def _ssd_out(yf, yb, xbc, proj, ssd_d, norm_g, n_rows, tm):
    w = SSD_INNER
    row = lambda i: (i, 0)
    return pl.pallas_call(
        _ssd_out_kernel,
        grid=(n_rows // tm,),
        in_specs=[pl.BlockSpec((tm, w), row), pl.BlockSpec((tm, w), row), pl.BlockSpec((tm, w), row),
                  pl.BlockSpec((tm, w), lambda i: (i, PC_Z // w)),
                  pl.BlockSpec((1, w), lambda i: (0, 0)), pl.BlockSpec((1, w), lambda i: (0, 0))],
        out_specs=pl.BlockSpec((tm, w), row),
        out_shape=jax.ShapeDtypeStruct((n_rows, w), BF16),
        compiler_params=_cparams(("parallel",)),
        name="ssd_out",
    )(yf, yb, xbc, proj, jnp.repeat(ssd_d, SSD_HEAD_DIM).reshape(1, w), norm_g.reshape(1, w))


def _rms(x, g, n):
    return x * lax.rsqrt(jnp.sum(x * x, axis=-1, keepdims=True) * (1.0 / n) + EPS) * g


def _rope(x, cos, sin_signed, half):
    w = x.shape[1]
    lane = lax.broadcasted_iota(jnp.int32, x.shape, 1)
    partner = jnp.where(lane % (2 * half) < half, pltpu.roll(x, w - half, 1), pltpu.roll(x, half, 1))
    return x * cos + partner * sin_signed


def _mla_prep_kernel(cq_ref, ckv_ref, kr_ref, gcq_ref, wuq_ref, gckv_ref, wukn_ref, wuv_ref, gq_ref, gk_ref,
                     cos_ref, sin_ref, q_ref, k_ref, v_ref):
    hq = _rms(cq_ref[...].astype(F32), gcq_ref[...], MLA_Q_LORA).astype(BF16)
    q_all = jnp.dot(hq, wuq_ref[...], preferred_element_type=F32)
    hkv = _rms(ckv_ref[...].astype(F32), gckv_ref[...], MLA_KV_LORA).astype(BF16)
    kn_all = jnp.dot(hkv, wukn_ref[...], preferred_element_type=F32)
    v_ref[...] = jnp.dot(hkv, wuv_ref[...], preferred_element_type=F32).astype(v_ref.dtype)
    kr = kr_ref[...].astype(F32)
    lane = lax.broadcasted_iota(jnp.int32, kr.shape, 1)
    kr = jnp.where(lane < MLA_ROPE, kr, 0.0)
    cos, sin = cos_ref[...], sin_ref[...]
    w = MLA_QK_PAD
    for h in range(MLA_HEADS):
        qh = _rms(q_all[:, h * w:(h + 1) * w], gq_ref[...], MLA_QK)
        q_ref[:, h * w:(h + 1) * w] = (_rope(qh, cos, sin, MLA_ROPE // 4) * MLA_SCALE).astype(q_ref.dtype)
        kh = jnp.concatenate([kn_all[:, h * MLA_NOPE:(h + 1) * MLA_NOPE], kr], axis=1)
        kh = _rms(kh, gk_ref[...], MLA_QK)
        k_ref[:, h * w:(h + 1) * w] = _rope(kh, cos, sin, MLA_ROPE // 4).astype(k_ref.dtype)


def _mla_prep(proj, p, cos, sin, n_rows, tm):
    hw = MLA_HEADS * MLA_QK_PAD
    hv = MLA_HEADS * MLA_V
    full = lambda i: (0, 0)
    return pl.pallas_call(
        _mla_prep_kernel,
        grid=(n_rows // tm,),
        in_specs=[pl.BlockSpec((tm, MLA_Q_LORA), lambda i: (i, PC_CQ // MLA_Q_LORA)),
                  pl.BlockSpec((tm, MLA_KV_LORA), lambda i: (i, PC_CKV // MLA_KV_LORA)),
                  pl.BlockSpec((tm, LANES), lambda i: (i, PC_KR // LANES)),
                  pl.BlockSpec((1, MLA_Q_LORA), full), pl.BlockSpec((MLA_Q_LORA, hw), full),
                  pl.BlockSpec((1, MLA_KV_LORA), full), pl.BlockSpec((MLA_KV_LORA, hv), full),
                  pl.BlockSpec((MLA_KV_LORA, hv), full),
                  pl.BlockSpec((1, MLA_QK_PAD), full), pl.BlockSpec((1, MLA_QK_PAD), full),
                  pl.BlockSpec((tm, MLA_QK_PAD), lambda i: (i, 0)),
                  pl.BlockSpec((tm, MLA_QK_PAD), lambda i: (i, 0))],
        out_specs=[pl.BlockSpec((tm, hw), lambda i: (i, 0)), pl.BlockSpec((tm, hw), lambda i: (i, 0)),
                   pl.BlockSpec((tm, hv), lambda i: (i, 0))],
        out_shape=[jax.ShapeDtypeStruct((n_rows, hw), BF16), jax.ShapeDtypeStruct((n_rows, hw), BF16),
                   jax.ShapeDtypeStruct((n_rows, hv), BF16)],
        compiler_params=_cparams(("parallel",)),
        name="mla_prep",
    )(proj, proj, proj, p['g_cq'], p['w_uq'], p['g_ckv'], p['w_ukn'], p['w_uv'], p['g_q'], p['g_k'], cos, sin)


def _diff_prep_kernel(q_in, k_in, gq_ref, gk_ref, cos_ref, sin_ref, q_ref, k_ref):
    cos, sin = cos_ref[...], sin_ref[...]
    w = DIFF_QK
    for c in range(2 * DIFF_HEADS):
        sl = slice(c * w, (c + 1) * w)
        qc = _rope(_rms(q_in[:, sl].astype(F32), gq_ref[...], w), cos, sin, w // 4)
        q_ref[:, sl] = (qc * DIFF_SCALE).astype(q_ref.dtype)
        kc = _rope(_rms(k_in[:, sl].astype(F32), gk_ref[...], w), cos, sin, w // 4)
        k_ref[:, sl] = kc.astype(k_ref.dtype)


def _diff_prep(proj, g_q, g_k, cos, sin, n_rows, tm):
    w = 2 * DIFF_HEADS * DIFF_QK
    full = lambda i: (0, 0)
    row = lambda i: (i, 0)
    return pl.pallas_call(
        _diff_prep_kernel,
        grid=(n_rows // tm,),
        in_specs=[pl.BlockSpec((tm, w), lambda i: (i, PC_DQ // w)),
                  pl.BlockSpec((tm, w), lambda i: (i, PC_DK // w)),
                  pl.BlockSpec((1, DIFF_QK), full), pl.BlockSpec((1, DIFF_QK), full),
                  pl.BlockSpec((tm, DIFF_QK), row), pl.BlockSpec((tm, DIFF_QK), row)],
        out_specs=[pl.BlockSpec((tm, w), row), pl.BlockSpec((tm, w), row)],
        out_shape=[jax.ShapeDtypeStruct((n_rows, w), BF16)] * 2,
        compiler_params=_cparams(("parallel",)),
        name="diff_prep",
    )(proj, proj, g_q.reshape(1, DIFF_QK), g_k.reshape(1, DIFF_QK), cos, sin)


def _softmax_step(q, k, v, carry):
    m, l, acc = carry
    s = lax.dot_general(q, k, (((1,), (1,)), ((), ())), preferred_element_type=F32)
    m_new = jnp.maximum(m, jnp.max(s, axis=1, keepdims=True))
    alpha = jnp.exp(m - m_new)
    p = jnp.exp(s - m_new)
    l = alpha * l + jnp.sum(p, axis=1, keepdims=True)
    acc = alpha * acc + jnp.dot(p.astype(BF16), v, preferred_element_type=F32)
    return m_new, l, acc


def _softmax_init(tq, dv):
    return (jnp.full((tq, 1), -jnp.inf, F32), jnp.zeros((tq, 1), F32), jnp.zeros((tq, dv), F32))


def _mla_attn_kernel(*refs, n_lat, tk):
    if n_lat:
        q_ref, kc_ref, vc_ref, kl_ref, vl_ref, o_ref = refs
    else:
        q_ref, kc_ref, vc_ref, o_ref = refs
    q = q_ref[...]
    carry = _softmax_step(q, kc_ref[...], vc_ref[...], _softmax_init(q.shape[0], MLA_V))
    if n_lat:
        def body(j, c):
            r = pl.multiple_of(j * tk, tk)
            return _softmax_step(q, kl_ref[pl.ds(r, tk), :], vl_ref[pl.ds(r, tk), :], c)
        carry = lax.fori_loop(0, n_lat, body, carry)
    _, l, acc = carry
    o_ref[...] = (acc / l).astype(o_ref.dtype)


def _diff_attn_kernel(*refs, n_lat, tk, lam_init):
    if n_lat:
        q_ref, kc_ref, vc_ref, kl_ref, vl_ref, lam_ref, g_ref, o_ref = refs
    else:
        q_ref, kc_ref, vc_ref, lam_ref, g_ref, o_ref = refs
    w = DIFF_QK
    q = q_ref[...]
    tq = q.shape[0]
    qs = (q[:, 0:w], q[:, w:2 * w])

    def step(k, v, carries):
        return tuple(_softmax_step(qs[c], k[:, c * w:(c + 1) * w], v, carries[c]) for c in range(2))

    carries = step(kc_ref[...], vc_ref[...], (_softmax_init(tq, DIFF_V), _softmax_init(tq, DIFF_V)))
    if n_lat:
        def body(j, c):
            r = pl.multiple_of(j * tk, tk)
            return step(kl_ref[pl.ds(r, tk), :], vl_ref[pl.ds(r, tk), :], c)
        carries = lax.fori_loop(0, n_lat, body, carries)
    lv = lam_ref[...]
    lam = (jnp.exp(jnp.sum(lv[0:1, :] * lv[1:2, :], axis=1, keepdims=True))
           - jnp.exp(jnp.sum(lv[2:3, :] * lv[3:4, :], axis=1, keepdims=True)) + lam_init)
    (_, l0, a0), (_, l1, a1) = carries
    o = a0 / l0 - lam * (a1 / l1)
    o = o * lax.rsqrt(jnp.mean(o * o, axis=-1, keepdims=True) + EPS) * g_ref[...] * (1.0 - lam_init)
    o_ref[...] = o.astype(o_ref.dtype)


def _attention(kernel, q, k, v, extra, heads, qw, vw, v_col0, batch, seq, ctx_len, latent, tq, tk):
    lat_ctx_blocks = batch * seq // ctx_len
    ctx_spec = lambda width, c0: pl.BlockSpec((ctx_len, width), lambda b, h, i: (lat_ctx_blocks + b, c0 + h))
    if latent:
        q_spec = pl.BlockSpec((tq, qw), lambda b, h, i: (b * (seq // tq) + i, h))
        in_specs = [q_spec, ctx_spec(qw, 0), ctx_spec(vw, v_col0),
                    pl.BlockSpec((seq, qw), lambda b, h, i: (b, h)),
                    pl.BlockSpec((seq, vw), lambda b, h, i: (b, v_col0 + h))]
        args = [q, k, v, k, v]
        grid = (batch, heads, seq // tq)
        out_spec = pl.BlockSpec((tq, vw), lambda b, h, i: (b * (seq // tq) + i, h))
        out_rows = batch * seq
        n_lat = seq // tk
    else:
        in_specs = [ctx_spec(qw, 0), ctx_spec(qw, 0), ctx_spec(vw, v_col0)]
        args = [q, k, v]
        grid = (batch, heads, 1)
        out_spec = pl.BlockSpec((ctx_len, vw), lambda b, h, i: (b, h))
        out_rows = batch * ctx_len
        n_lat = 0
    for e in extra:
        in_specs.append(pl.BlockSpec(e.shape, lambda b, h, i: (0, 0)))
        args.append(e)
    return pl.pallas_call(
        functools.partial(kernel, n_lat=n_lat, tk=tk),
        grid=grid,
        in_specs=in_specs,
        out_specs=out_spec,
        out_shape=jax.ShapeDtypeStruct((out_rows, heads * vw), BF16),
        compiler_params=_cparams(("parallel", "parallel", "arbitrary")),
        name="attn_latent" if latent else "attn_ctx",
    )(*args)


def _rope_tables(rows, rot_dim):
    axis_dim = rot_dim // 2
    inv = ROPE_THETA ** (-jnp.arange(0, axis_dim, 2, dtype=F32) / axis_dim)
    row = jnp.repeat(jnp.arange(rows, dtype=F32), GRID_W)
    colv = (jnp.arange(rows * GRID_W) % GRID_W).astype(F32)
    ar, ac = row[:, None] * inv, colv[:, None] * inv
    cos = jnp.concatenate([jnp.cos(ar), jnp.cos(ar), jnp.cos(ac), jnp.cos(ac)], axis=1)
    sin = jnp.concatenate([-jnp.sin(ar), jnp.sin(ar), -jnp.sin(ac), jnp.sin(ac)], axis=1)
    return cos, sin


def _token_tables(cos, sin, batch, ctx_rows, left, right):
    n = cos.shape[0]
    cos = jnp.concatenate([jnp.ones((n, left), F32), cos, jnp.ones((n, right), F32)], axis=1)
    sin = jnp.concatenate([jnp.zeros((n, left), F32), sin, jnp.zeros((n, right), F32)], axis=1)
    w = cos.shape[1]
    cos = jnp.concatenate([jnp.tile(cos, (batch, 1)), jnp.ones((ctx_rows, w), F32)], axis=0)
    sin = jnp.concatenate([jnp.tile(sin, (batch, 1)), jnp.zeros((ctx_rows, w), F32)], axis=0)
    return cos, sin


def _layout_w_in(w):
    d = w.shape[0]
    s1, s2 = SSD_COLS, SSD_COLS + MLA_COLS
    xbc0 = SSD_INNER
    z = w[:, :SSD_INNER]
    xs = w[:, xbc0:xbc0 + SSD_INNER]
    bm = w[:, xbc0 + SSD_INNER:xbc0 + SSD_INNER + SSD_GN]
    cm = w[:, xbc0 + SSD_INNER + SSD_GN:xbc0 + SSD_CONV_CH]
    dt = w[:, xbc0 + SSD_CONV_CH:s1]
    cq = w[:, s1:s1 + MLA_Q_LORA]
    ckv = w[:, s1 + MLA_Q_LORA:s1 + MLA_Q_LORA + MLA_KV_LORA]
    kr = w[:, s1 + MLA_Q_LORA + MLA_KV_LORA:s2]
    qk = 2 * DIFF_HEADS * DIFF_QK
    dq, dk, dv = w[:, s2:s2 + qk], w[:, s2 + qk:s2 + 2 * qk], w[:, s2 + 2 * qk:]
    zeros = lambda n: jnp.zeros((d, n), w.dtype)
    main = jnp.concatenate([z, xs, dq, dk, dv, cq, bm, cm, ckv, kr, zeros(LANES - MLA_ROPE), zeros(LANES)], axis=1)
    dtw = jnp.concatenate([dt, zeros(LANES - 2 * SSD_HEADS)], axis=1)
    return main.astype(BF16), dtw.astype(BF16)


def _layout_mla(w_uq, w_ukv, g_q, g_k):
    pad = MLA_QK_PAD - MLA_QK
    uq = jnp.pad(w_uq.reshape(MLA_Q_LORA, MLA_HEADS, MLA_QK), ((0, 0), (0, 0), (0, pad)))
    ukv = w_ukv.reshape(MLA_KV_LORA, MLA_HEADS, MLA_NOPE + MLA_V)
    return dict(
        w_uq=uq.reshape(MLA_Q_LORA, MLA_HEADS * MLA_QK_PAD).astype(BF16),
        w_ukn=ukv[:, :, :MLA_NOPE].reshape(MLA_KV_LORA, MLA_HEADS * MLA_NOPE).astype(BF16),
        w_uv=ukv[:, :, MLA_NOPE:].reshape(MLA_KV_LORA, MLA_HEADS * MLA_V).astype(BF16),
        g_q=jnp.pad(g_q, (0, pad)).reshape(1, MLA_QK_PAD),
        g_k=jnp.pad(g_k, (0, pad)).reshape(1, MLA_QK_PAD),
    )


def kernel(x, c, ctx, c_ctx, w_mod, b_mod, g_norm, w_ffn_gate, w_ffn_up, w_ffn_down, w_in, ssd_conv_w, ssd_conv_b, ssd_dt_bias, ssd_a_log, ssd_d, ssd_norm_g, mla_g_cq, mla_w_uq, mla_g_ckv, mla_w_ukv, mla_g_q, mla_g_k, diff_g_q, diff_g_k, diff_lambda, diff_subln_g, w_out):
    batch, seq, d = x.shape
    ctx_len = ctx.shape[1]
    depth = w_mod.shape[0]
    d_ff = w_ffn_gate.shape[-1]
    lat_rows, ctx_rows = batch * seq, batch * ctx_len
    n_rows = lat_rows + ctx_rows
    assert w_in.shape[-1] == IN_COLS and w_out.shape[1] == SSD_INNER + MLA_HEADS * MLA_V + DIFF_HEADS * DIFF_V
    assert seq % GRID_W == 0 and batch < MOD_ROWS
    tm = _pick(math.gcd(seq, ctx_rows), (512, 256, 128))
    tr = _pick(math.gcd(seq, ctx_len), (256, 128))
    assert seq % tm == 0 and ctx_rows % tm == 0 and seq % tr == 0 and ctx_len % tr == 0
    assert ctx_len % SSD_CHUNK == 0 and seq % SSD_CHUNK == 0 and lat_rows % ctx_len == 0
    tq = _pick(seq, (512, 256, 128))
    tk = _pick(seq, (512, 256, 128))

    def group_of_row(r):
        return jnp.minimum(r // seq, batch)

    xs = jnp.concatenate([x.reshape(lat_rows, d), ctx.reshape(ctx_rows, d)], axis=0)
    c_rows = jnp.concatenate([c, c_ctx[None, :], jnp.zeros((MOD_ROWS - batch - 1, d), F32)], axis=0)
    mod = _modulation(c_rows, w_mod, b_mod).reshape(depth, MOD_ROWS, 1, N_MOD * d)

    rows = seq // GRID_W
    cos_m, sin_m = _token_tables(*_rope_tables(rows, MLA_ROPE), batch, ctx_rows, MLA_NOPE,
                                 MLA_QK_PAD - MLA_QK)
    cos_d, sin_d = _token_tables(*_rope_tables(rows, DIFF_QK), batch, ctx_rows, 0, 0)

    tn_ff = _pick(d_ff, (512, 256, 128))
    tn_d = _pick(d, (512, 256, 128))
    tn_in = _pick(PROJ_COLS, (768,))

    def ffn(xs, li, j, norm_idx, gate_idx, m_rows):
        h = _norm_mod(xs, g_norm[li, norm_idx], mod[li], norm_idx, m_rows, group_of_row, tr)
        act = _matmul_swiglu(h, w_ffn_gate[li, j].astype(BF16), w_ffn_up[li, j].astype(BF16), m_rows, tm, tn_ff)
        return _matmul_resid([(act, w_ffn_down[li, j].astype(BF16))], xs, mod[li], gate_idx, 0.5, m_rows,
                             group_of_row, tm, tn_d)

    for li in range(depth):
        need_ctx = li < depth - 1
        out_rows = n_rows if need_ctx else lat_rows
        lam_init = 0.8 - 0.6 * math.exp(-0.3 * li)

        xs = ffn(xs, li, 0, 0, 2, n_rows)

        h = _norm_mod(xs, g_norm[li, 1], mod[li], 1, n_rows, group_of_row, tr)
        w_main, w_dt = _layout_w_in(w_in[li])
        proj = _matmul(h, w_main, n_rows, BF16, tm, tn_in)
        dt_raw = _matmul(h, w_dt, n_rows, F32, tm, LANES)

        xbc = _ssd_conv(proj, ssd_conv_w[li], ssd_conv_b[li], n_rows, lat_rows, seq, ctx_len, tr)
        yf, yb = _ssd_scan(xbc, dt_raw, ssd_dt_bias[li], ssd_a_log[li], batch, seq, ctx_len)
        ssd_o = _ssd_out(yf, yb, xbc, proj, ssd_d[li], ssd_norm_g[li], out_rows, tr)

        mla_p = _layout_mla(mla_w_uq[li], mla_w_ukv[li], mla_g_q[li], mla_g_k[li])
        mla_p['g_cq'] = mla_g_cq[li].reshape(1, MLA_Q_LORA)
        mla_p['g_ckv'] = mla_g_ckv[li].reshape(1, MLA_KV_LORA)
        mq, mk, mv = _mla_prep(proj, mla_p, cos_m, sin_m, n_rows, tr)
        mla_args = (_mla_attn_kernel, mq, mk, mv, [], MLA_HEADS, MLA_QK_PAD, MLA_V, 0, batch, seq, ctx_len)
        mla_o = _attention(*mla_args, True, tq, tk)

        dq, dk = _diff_prep(proj, diff_g_q[li], diff_g_k[li], cos_d, sin_d, n_rows, tr)
        diff_kernel = functools.partial(_diff_attn_kernel, lam_init=lam_init)
        extra = [diff_lambda[li], diff_subln_g[li].reshape(1, DIFF_V)]
        diff_args = (diff_kernel, dq, dk, proj, extra, DIFF_HEADS, 2 * DIFF_QK, DIFF_V, PC_DV // DIFF_V,
                     batch, seq, ctx_len)
        diff_o = _attention(*diff_args, True, tq, tk)
        if need_ctx:
            mla_o = jnp.concatenate([mla_o, _attention(*mla_args, False, tq, tk)], axis=0)
            diff_o = jnp.concatenate([diff_o, _attention(*diff_args, False, tq, tk)], axis=0)

        wo = w_out[li].astype(BF16)
        o1, o2 = SSD_INNER, SSD_INNER + MLA_HEADS * MLA_V
        xs = _matmul_resid([(ssd_o, wo[:o1]), (mla_o, wo[o1:o2]), (diff_o, wo[o2:])], xs, mod[li], 5, 1.0,
                           out_rows, group_of_row, tm, tn_d)

        xs = ffn(xs, li, 1, 2, 8, out_rows)

    return xs[:lat_rows].reshape(batch, seq, d)
```

```python
import functools
import math

import jax
import jax.numpy as jnp
from jax import lax
from jax.experimental import pallas as pl
from jax.experimental.pallas import tpu as pltpu

F32 = jnp.float32
BF16 = jnp.bfloat16

GRID_W = 64
EPS = 1e-6
ROPE_THETA = 10000.0
N_MOD = 9
SSD_INNER = 2048
SSD_HEAD_DIM = 64
SSD_HEADS = 32
SSD_GROUPS = 4
SSD_STATE = 128
SSD_CONV = 5
SSD_CHUNK = 128
SSD_GN = SSD_GROUPS * SSD_STATE
SSD_CONV_CH = SSD_INNER + 2 * SSD_GN
SSD_COLS = 2 * SSD_INNER + 2 * SSD_GN + 2 * SSD_HEADS
MLA_HEADS = 8
MLA_NOPE = 128
MLA_ROPE = 64
MLA_V = 128
MLA_Q_LORA = 1024
MLA_KV_LORA = 512
MLA_QK = MLA_NOPE + MLA_ROPE
MLA_QK_PAD = 256
MLA_SCALE = 1.0 / math.sqrt(MLA_QK)
LOG2E = math.log2(math.e)
MLA_COLS = MLA_Q_LORA + MLA_KV_LORA + MLA_ROPE
DIFF_HEADS = 4
DIFF_QK = 128
DIFF_V = 2 * DIFF_QK
DIFF_SCALE = 1.0 / math.sqrt(DIFF_QK)
DIFF_COLS = DIFF_HEADS * (4 * DIFF_QK + DIFF_V)
IN_COLS = SSD_COLS + MLA_COLS + DIFF_COLS

LANES = 128
BF16_SUBLANES = 16
MOD_ROWS = 8
VMEM_LIMIT = 50 * 1024 * 1024

PC_Z = 0
PC_XS = 2048
PC_DQ = 4096
PC_DK = 5120
PC_DV = 6144
PC_CQ = 7168
PC_B = 8192
PC_C = 8704
PC_CKV = 9216
PC_KR = 9728
PC_PAD = 9856
PROJ_COLS = 9984


def _cparams(sem):
    return pltpu.CompilerParams(dimension_semantics=sem, vmem_limit_bytes=VMEM_LIMIT)


def _pick(n, prefs):
    for p in prefs:
        if n % p == 0:
            return p
    return n


def _silu(x):
    return x / (1.0 + jnp.exp(-x))


def _mm_kernel(a_ref, b_ref, o_ref):
    o_ref[...] = jnp.dot(a_ref[...], b_ref[...], preferred_element_type=F32).astype(o_ref.dtype)


def _matmul(a, b, m_rows, out_dtype, tm, tn):
    k = a.shape[1]
    n = b.shape[1]
    return pl.pallas_call(
        _mm_kernel,
        grid=(n // tn, m_rows // tm),
        in_specs=[pl.BlockSpec((tm, k), lambda j, i: (i, 0)),
                  pl.BlockSpec((k, tn), lambda j, i: (0, j))],
        out_specs=pl.BlockSpec((tm, tn), lambda j, i: (i, j)),
        out_shape=jax.ShapeDtypeStruct((m_rows, n), out_dtype),
        compiler_params=_cparams(("parallel", "parallel")),
        name="mm_plain",
    )(a, b)


def _swiglu_kernel(a_ref, wg_ref, wu_ref, o_ref):
    a = a_ref[...]
    g = jnp.dot(a, wg_ref[...], preferred_element_type=F32)
    u = jnp.dot(a, wu_ref[...], preferred_element_type=F32)
    o_ref[...] = (_silu(g) * u).astype(o_ref.dtype)


def _matmul_swiglu(a, wg, wu, m_rows, tm, tn):
    k = a.shape[1]
    n = wg.shape[1]
    return pl.pallas_call(
        _swiglu_kernel,
        grid=(n // tn, m_rows // tm),
        in_specs=[pl.BlockSpec((tm, k), lambda j, i: (i, 0)),
                  pl.BlockSpec((k, tn), lambda j, i: (0, j)),
                  pl.BlockSpec((k, tn), lambda j, i: (0, j))],
        out_specs=pl.BlockSpec((tm, tn), lambda j, i: (i, j)),
        out_shape=jax.ShapeDtypeStruct((m_rows, n), BF16),
        compiler_params=_cparams(("parallel", "parallel")),
        name="mm_swiglu",
    )(a, wg, wu)


def _resid_kernel(*refs, n_pairs, scale):
    x_ref, gate_ref, o_ref = refs[2 * n_pairs], refs[2 * n_pairs + 1], refs[2 * n_pairs + 2]
    acc = jnp.dot(refs[0][...], refs[1][...], preferred_element_type=F32)
    for p in range(1, n_pairs):
        acc = acc + jnp.dot(refs[2 * p][...], refs[2 * p + 1][...], preferred_element_type=F32)
    o_ref[...] = x_ref[...] + (scale * gate_ref[...]) * acc


def _matmul_resid(pairs, x, mod4, gate_idx, scale, m_rows, group_of_row, tm, tn):
    d = x.shape[1]
    in_specs = []
    args = []
    for a, b in pairs:
        in_specs.append(pl.BlockSpec((tm, a.shape[1]), lambda j, i: (i, 0)))
        in_specs.append(pl.BlockSpec((b.shape[0], tn), lambda j, i: (0, j)))
        args += [a, b]
    in_specs.append(pl.BlockSpec((tm, tn), lambda j, i: (i, j)))
    gate_blk = gate_idx * (d // tn)
    in_specs.append(pl.BlockSpec((None, 1, tn), lambda j, i: (group_of_row(i * tm), 0, gate_blk + j)))
    args += [x, mod4]
    return pl.pallas_call(
        functools.partial(_resid_kernel, n_pairs=len(pairs), scale=scale),
        grid=(d // tn, m_rows // tm),
        in_specs=in_specs,
        out_specs=pl.BlockSpec((tm, tn), lambda j, i: (i, j)),
        out_shape=jax.ShapeDtypeStruct((m_rows, d), F32),
        compiler_params=_cparams(("parallel", "parallel")),
        name="mm_resid",
    )(*args)


def _mod_kernel(c_ref, w_ref, b_ref, o_ref):
    s = _silu(c_ref[...]).astype(BF16)
    w = w_ref[...].astype(BF16)
    o_ref[...] = jnp.dot(s, w, preferred_element_type=F32) + b_ref[...]


def _modulation(c_rows, w_mod, b_mod):
    depth, d, n = w_mod.shape
    tn = _pick(n, (512, 256, 128))
    return pl.pallas_call(
        _mod_kernel,
        grid=(depth, n // tn),
        in_specs=[pl.BlockSpec((MOD_ROWS, d), lambda l, j: (0, 0)),
                  pl.BlockSpec((None, d, tn), lambda l, j: (l, 0, j)),
                  pl.BlockSpec((None, 1, tn), lambda l, j: (l, 0, j))],
        out_specs=pl.BlockSpec((None, MOD_ROWS, tn), lambda l, j: (l, 0, j)),
        out_shape=jax.ShapeDtypeStruct((depth, MOD_ROWS, n), F32),
        compiler_params=_cparams(("parallel", "parallel")),
        name="modulation",
    )(c_rows, w_mod, b_mod.reshape(depth, 1, n))


def _norm_mod_kernel(x_ref, g_ref, shift_ref, scale_ref, o_ref):
    x = x_ref[...]
    y = x * lax.rsqrt(jnp.mean(x * x, axis=-1, keepdims=True) + EPS) * g_ref[...]
    o_ref[...] = (y * (1.0 + scale_ref[...]) + shift_ref[...]).astype(o_ref.dtype)


def _norm_mod(x, g, mod4, idx, m_rows, group_of_row, tm):
    d = x.shape[1]
    return pl.pallas_call(
        _norm_mod_kernel,
        grid=(m_rows // tm,),
        in_specs=[pl.BlockSpec((tm, d), lambda i: (i, 0)),
                  pl.BlockSpec((1, d), lambda i: (0, 0)),
                  pl.BlockSpec((None, 1, d), lambda i: (group_of_row(i * tm), 0, 3 * idx)),
                  pl.BlockSpec((None, 1, d), lambda i: (group_of_row(i * tm), 0, 3 * idx + 1))],
        out_specs=pl.BlockSpec((tm, d), lambda i: (i, 0)),
        out_shape=jax.ShapeDtypeStruct((m_rows, d), BF16),
        compiler_params=_cparams(("parallel",)),
        name="norm_mod",
    )(x, g.reshape(1, d), mod4, mod4)


def _conv_kernel(prev_ref, cur_ref, next_ref, w_ref, b_ref, o_ref, ext_ref, *, tr, lat_rows, seq, ctx_len):
    r0 = pl.program_id(0) * tr
    in_lat = r0 < lat_rows
    off = jnp.where(in_lat, r0 % seq, (r0 - lat_rows) % ctx_len)
    seg = jnp.where(in_lat, seq, ctx_len)
    first = off == 0
    last = off + tr == seg
    h = BF16_SUBLANES
    ext_ref[0:h, :] = jnp.where(first, 0.0, prev_ref[...].astype(F32))
    ext_ref[h:h + tr, :] = cur_ref[...].astype(F32)
    ext_ref[h + tr:2 * h + tr, :] = jnp.where(last, 0.0, next_ref[...].astype(F32))
    acc = b_ref[...] + w_ref[0:1, :] * ext_ref[pl.ds(h - 2, tr), :]
    for k in range(1, SSD_CONV):
        acc = acc + w_ref[k:k + 1, :] * ext_ref[pl.ds(h - 2 + k, tr), :]
    o_ref[...] = _silu(acc).astype(o_ref.dtype)


def _ssd_conv(proj, conv_w, conv_b, n_rows, lat_rows, seq, ctx_len, tr):
    tc = 512
    h = BF16_SUBLANES
    rb = tr // h
    n_halo = n_rows // h
    xs_blk, bc_blk = PC_XS // tc, PC_B // tc
    n_xs = SSD_INNER // tc

    def col(j):
        return jnp.where(j < n_xs, xs_blk + j, bc_blk + (j - n_xs))

    return pl.pallas_call(
        functools.partial(_conv_kernel, tr=tr, lat_rows=lat_rows, seq=seq, ctx_len=ctx_len),
        grid=(n_rows // tr, SSD_CONV_CH // tc),
        in_specs=[pl.BlockSpec((h, tc), lambda i, j: (jnp.maximum(i * rb - 1, 0), col(j))),
                  pl.BlockSpec((tr, tc), lambda i, j: (i, col(j))),
                  pl.BlockSpec((h, tc), lambda i, j: (jnp.minimum((i + 1) * rb, n_halo - 1), col(j))),
                  pl.BlockSpec((SSD_CONV, tc), lambda i, j: (0, j)),
                  pl.BlockSpec((1, tc), lambda i, j: (0, j))],
        out_specs=pl.BlockSpec((tr, tc), lambda i, j: (i, j)),
        out_shape=jax.ShapeDtypeStruct((n_rows, SSD_CONV_CH), BF16),
        scratch_shapes=[pltpu.VMEM((tr + 2 * h, tc), F32)],
        compiler_params=_cparams(("parallel", "parallel")),
        name="ssd_conv",
    )(proj, proj, proj, conv_w, conv_b.reshape(1, SSD_CONV_CH))


N_PAIRS = SSD_HEADS // 2
PAIRS_PER_GROUP = N_PAIRS // SSD_GROUPS


def _ssd_direction(d, x_ref, b_ref, c_ref, dt_ref, bias_ref, alog_ref, y_ref, st_ref,
                   cs_ref, cst_ref, dtt_ref, woutt_ref, g_ref, bt_ref, yoff_ref, xp_ref, yp_ref):
    q = SSD_CHUNK
    lane = lax.broadcasted_iota(jnp.int32, (q, LANES), 1)
    row = lax.broadcasted_iota(jnp.int32, (q, q), 0)
    col = lax.broadcasted_iota(jnp.int32, (q, q), 1)
    mask = (row >= col) if d == 0 else (row <= col)

    z = dt_ref[...] + bias_ref[...]
    dt = jnp.maximum(z, 0.0) + jnp.log1p(jnp.exp(-jnp.abs(z)))
    a_dt = dt * (-jnp.exp(alog_ref[...]))
    hi = a_dt.astype(BF16)
    r1 = a_dt - hi.astype(F32)
    mid = r1.astype(BF16)
    lo = (r1 - mid.astype(F32)).astype(BF16)
    tri = jnp.where(mask, 1.0, 0.0).astype(BF16)
    c3 = jnp.dot(tri, jnp.concatenate([hi, mid, lo], axis=1), preferred_element_type=F32)
    cs = c3[:, 0:LANES] + c3[:, LANES:2 * LANES] + c3[:, 2 * LANES:3 * LANES]
    end = q - 1 if d == 0 else 0
    a_tot = cs[end:end + 1, :]
    cs_ref[...] = cs
    cst_ref[...] = cs.T
    dtt_ref[...] = dt.T
    woutt_ref[...] = (dt * jnp.exp(a_tot - cs)).T

    for g in range(SSD_GROUPS):
        bg = b_ref[:, g * SSD_STATE:(g + 1) * SSD_STATE]
        cg = c_ref[:, g * SSD_STATE:(g + 1) * SSD_STATE]
        g_ref[g] = lax.dot_general(cg, bg, (((1,), (1,)), ((), ())), preferred_element_type=F32)
        bt_ref[g] = bg.astype(F32).T
        for pp in range(PAIRS_PER_GROUP):
            p = g * PAIRS_PER_GROUP + pp
            yoff_ref[p] = jnp.dot(cg, st_ref[p].astype(BF16), preferred_element_type=F32)
    for p in range(N_PAIRS):
        xp_ref[p] = x_ref[:, p * LANES:(p + 1) * LANES]

    half = lane < SSD_HEAD_DIM

    def pair_body(p, carry):
        g = p // PAIRS_PER_GROUP
        h0 = 2 * p + SSD_HEADS * d
        gmat = g_ref[g]
        bt = bt_ref[g]
        xp = xp_ref[p]
        ys, ups, dins = [], [], []
        for hh in (h0, h0 + 1):
            cs_col = jnp.sum(jnp.where(lane == hh, cs_ref[...], 0.0), axis=1, keepdims=True)
            lmat = jnp.exp(jnp.where(mask, cs_col - cst_ref[pl.ds(hh, 1), :], -1e30))
            m = (gmat * lmat * dtt_ref[pl.ds(hh, 1), :]).astype(BF16)
            ys.append(jnp.dot(m, xp, preferred_element_type=F32))
            bw = (bt * woutt_ref[pl.ds(hh, 1), :]).astype(BF16)
            ups.append(jnp.dot(bw, xp, preferred_element_type=F32))
            dins.append(jnp.exp(cs_col))
        din = jnp.where(half, dins[0], dins[1])
        yp_ref[p] = jnp.where(half, ys[0], ys[1]) + yoff_ref[p] * din
        st_ref[p] = st_ref[p] * din[end:end + 1, :] + jnp.where(half, ups[0], ups[1])
        return carry

    lax.fori_loop(0, N_PAIRS, pair_body, 0, unroll=4)
    for p in range(N_PAIRS):
        y_ref[:, p * LANES:(p + 1) * LANES] = yp_ref[p]


def _ssd_kernel(xf_ref, bf_ref, cf_ref, dtf_ref, xb_ref, bb_ref, cb_ref, dtb_ref, bias_ref, alog_ref,
                yf_ref, yb_ref, stf_ref, stb_ref, cs_ref, cst_ref, dtt_ref, woutt_ref,
                g_ref, bt_ref, yoff_ref, xp_ref, yp_ref):
    @pl.when(pl.program_id(1) == 0)
    def _():
        stf_ref[...] = jnp.zeros_like(stf_ref)
        stb_ref[...] = jnp.zeros_like(stb_ref)

    shared = (cs_ref, cst_ref, dtt_ref, woutt_ref, g_ref, bt_ref, yoff_ref, xp_ref, yp_ref)
    _ssd_direction(0, xf_ref, bf_ref, cf_ref, dtf_ref, bias_ref, alog_ref, yf_ref, stf_ref, *shared)
    _ssd_direction(1, xb_ref, bb_ref, cb_ref, dtb_ref, bias_ref, alog_ref, yb_ref, stb_ref, *shared)


def _ssd_scan(xbc, dt_raw, dt_bias, a_log, batch, seq, ctx_len):
    q = SSD_CHUNK
    n_rows = xbc.shape[0]
    ncc, nlc = ctx_len // q, seq // q
    lat_blocks = batch * seq // q

    def fwd_blk(b, s):
        return jnp.where(s < ncc, lat_blocks + b * ncc + s, b * nlc + (s - ncc))

    def bwd_blk(b, s):
        return jnp.where(s < ncc, lat_blocks + b * ncc + (ncc - 1 - s), b * nlc + (nlc - 1 - (s - ncc)))

    def specs(blk):
        return [pl.BlockSpec((q, SSD_INNER), lambda b, s: (blk(b, s), 0)),
                pl.BlockSpec((q, SSD_GN), lambda b, s: (blk(b, s), SSD_INNER // SSD_GN)),
                pl.BlockSpec((q, SSD_GN), lambda b, s: (blk(b, s), SSD_INNER // SSD_GN + 1)),
                pl.BlockSpec((q, LANES), lambda b, s: (blk(b, s), 0))]

    pad = jnp.zeros((LANES - 2 * SSD_HEADS,), F32)
    bias = jnp.concatenate([dt_bias.reshape(-1), pad]).reshape(1, LANES)
    alog = jnp.concatenate([a_log.reshape(-1), pad]).reshape(1, LANES)
    tile = (q, LANES)
    return pl.pallas_call(
        _ssd_kernel,
        grid=(batch, ncc + nlc),
        in_specs=specs(fwd_blk) + specs(bwd_blk) + [pl.BlockSpec((1, LANES), lambda b, s: (0, 0))] * 2,
        out_specs=[pl.BlockSpec((q, SSD_INNER), lambda b, s: (fwd_blk(b, s), 0)),
                   pl.BlockSpec((q, SSD_INNER), lambda b, s: (bwd_blk(b, s), 0))],
        out_shape=[jax.ShapeDtypeStruct((n_rows, SSD_INNER), F32)] * 2,
        scratch_shapes=[pltpu.VMEM((N_PAIRS, SSD_STATE, LANES), F32),
                        pltpu.VMEM((N_PAIRS, SSD_STATE, LANES), F32),
                        pltpu.VMEM(tile, F32), pltpu.VMEM(tile, F32), pltpu.VMEM(tile, F32),
                        pltpu.VMEM(tile, F32),
                        pltpu.VMEM((SSD_GROUPS, q, q), F32),
                        pltpu.VMEM((SSD_GROUPS, SSD_STATE, q), F32),
                        pltpu.VMEM((N_PAIRS, q, LANES), F32),
                        pltpu.VMEM((N_PAIRS, q, LANES), BF16),
                        pltpu.VMEM((N_PAIRS, q, LANES), F32)],
        compiler_params=_cparams(("parallel", "arbitrary")),
        name="ssd_scan",
    )(xbc, xbc, xbc, dt_raw, xbc, xbc, xbc, dt_raw, bias, alog)


def _ssd_out_kernel(yf_ref, yb_ref, x_ref, z_ref, d_ref, g_ref, o_ref):
    z = z_ref[...].astype(F32)
    y = (yf_ref[...] + yb_ref[...] + d_ref[...] * x_ref[...].astype(F32)) * _silu(z)
    y = y * lax.rsqrt(jnp.mean(y * y, axis=-1, keepdims=True) + EPS) * g_ref[...]
    o_ref[...] = y.astype(o_ref.dtype)


def _ssd_out(yf, yb, xbc, proj, ssd_d, norm_g, n_rows, tm):
    w = SSD_INNER
    row = lambda i: (i, 0)
    return pl.pallas_call(
        _ssd_out_kernel,
        grid=(n_rows // tm,),
        in_specs=[pl.BlockSpec((tm, w), row), pl.BlockSpec((tm, w), row), pl.BlockSpec((tm, w), row),
                  pl.BlockSpec((tm, w), lambda i: (i, PC_Z // w)),
                  pl.BlockSpec((1, w), lambda i: (0, 0)), pl.BlockSpec((1, w), lambda i: (0, 0))],
        out_specs=pl.BlockSpec((tm, w), row),
        out_shape=jax.ShapeDtypeStruct((n_rows, w), BF16),
        compiler_params=_cparams(("parallel",)),
        name="ssd_out",
    )(yf, yb, xbc, proj, jnp.repeat(ssd_d, SSD_HEAD_DIM).reshape(1, w), norm_g.reshape(1, w))


def _rms(x, g, n):
    return x * lax.rsqrt(jnp.sum(x * x, axis=-1, keepdims=True) * (1.0 / n) + EPS) * g


def _rope(x, cos, sin_signed, half):
    w = x.shape[1]
    lane = lax.broadcasted_iota(jnp.int32, x.shape, 1)
    partner = jnp.where(lane % (2 * half) < half, pltpu.roll(x, w - half, 1), pltpu.roll(x, half, 1))
    return x * cos + partner * sin_signed


def _mla_prep_kernel(cq_ref, ckv_ref, kr_ref, gcq_ref, wuq_ref, gckv_ref, wukn_ref, wuv_ref, gq_ref, gk_ref,
                     cos_ref, sin_ref, q_ref, k_ref, v_ref):
    hq = _rms(cq_ref[...].astype(F32), gcq_ref[...], MLA_Q_LORA).astype(BF16)
    q_all = jnp.dot(hq, wuq_ref[...], preferred_element_type=F32)
    hkv = _rms(ckv_ref[...].astype(F32), gckv_ref[...], MLA_KV_LORA).astype(BF16)
    kn_all = jnp.dot(hkv, wukn_ref[...], preferred_element_type=F32)
    v_ref[...] = jnp.dot(hkv, wuv_ref[...], preferred_element_type=F32).astype(v_ref.dtype)
    kr = kr_ref[...].astype(F32)
    lane = lax.broadcasted_iota(jnp.int32, kr.shape, 1)
    kr = jnp.where(lane < MLA_ROPE, kr, 0.0)
    cos, sin = cos_ref[...], sin_ref[...]
    w = MLA_QK_PAD
    for h in range(MLA_HEADS):
        qh = _rms(q_all[:, h * w:(h + 1) * w], gq_ref[...], MLA_QK)
        q_ref[:, h * w:(h + 1) * w] = (_rope(qh, cos, sin, MLA_ROPE // 4) * (MLA_SCALE * LOG2E)).astype(q_ref.dtype)
        kh = jnp.concatenate([kn_all[:, h * MLA_NOPE:(h + 1) * MLA_NOPE], kr], axis=1)
        kh = _rms(kh, gk_ref[...], MLA_QK)
        k_ref[:, h * w:(h + 1) * w] = _rope(kh, cos, sin, MLA_ROPE // 4).astype(k_ref.dtype)


def _mla_prep(proj, p, cos, sin, n_rows, tm):
    hw = MLA_HEADS * MLA_QK_PAD
    hv = MLA_HEADS * MLA_V
    full = lambda i: (0, 0)
    return pl.pallas_call(
        _mla_prep_kernel,
        grid=(n_rows // tm,),
        in_specs=[pl.BlockSpec((tm, MLA_Q_LORA), lambda i: (i, PC_CQ // MLA_Q_LORA)),
                  pl.BlockSpec((tm, MLA_KV_LORA), lambda i: (i, PC_CKV // MLA_KV_LORA)),
                  pl.BlockSpec((tm, LANES), lambda i: (i, PC_KR // LANES)),
                  pl.BlockSpec((1, MLA_Q_LORA), full), pl.BlockSpec((MLA_Q_LORA, hw), full),
                  pl.BlockSpec((1, MLA_KV_LORA), full), pl.BlockSpec((MLA_KV_LORA, hv), full),
                  pl.BlockSpec((MLA_KV_LORA, hv), full),
                  pl.BlockSpec((1, MLA_QK_PAD), full), pl.BlockSpec((1, MLA_QK_PAD), full),
                  pl.BlockSpec((tm, MLA_QK_PAD), lambda i: (i, 0)),
                  pl.BlockSpec((tm, MLA_QK_PAD), lambda i: (i, 0))],
        out_specs=[pl.BlockSpec((tm, hw), lambda i: (i, 0)), pl.BlockSpec((tm, hw), lambda i: (i, 0)),
                   pl.BlockSpec((tm, hv), lambda i: (i, 0))],
        out_shape=[jax.ShapeDtypeStruct((n_rows, hw), BF16), jax.ShapeDtypeStruct((n_rows, hw), BF16),
                   jax.ShapeDtypeStruct((n_rows, hv), BF16)],
        compiler_params=_cparams(("parallel",)),
        name="mla_prep",
    )(proj, proj, proj, p['g_cq'], p['w_uq'], p['g_ckv'], p['w_ukn'], p['w_uv'], p['g_q'], p['g_k'], cos, sin)


def _diff_prep_kernel(q_in, k_in, gq_ref, gk_ref, cos_ref, sin_ref, q_ref, k_ref):
    cos, sin = cos_ref[...], sin_ref[...]
    w = DIFF_QK
    for c in range(2 * DIFF_HEADS):
        sl = slice(c * w, (c + 1) * w)
        qc = _rope(_rms(q_in[:, sl].astype(F32), gq_ref[...], w), cos, sin, w // 4)
        q_ref[:, sl] = (qc * (DIFF_SCALE * LOG2E)).astype(q_ref.dtype)
        kc = _rope(_rms(k_in[:, sl].astype(F32), gk_ref[...], w), cos, sin, w // 4)
        k_ref[:, sl] = kc.astype(k_ref.dtype)


def _diff_prep(proj, g_q, g_k, cos, sin, n_rows, tm):
    w = 2 * DIFF_HEADS * DIFF_QK
    full = lambda i: (0, 0)
    row = lambda i: (i, 0)
    return pl.pallas_call(
        _diff_prep_kernel,
        grid=(n_rows // tm,),
        in_specs=[pl.BlockSpec((tm, w), lambda i: (i, PC_DQ // w)),
                  pl.BlockSpec((tm, w), lambda i: (i, PC_DK // w)),
                  pl.BlockSpec((1, DIFF_QK), full), pl.BlockSpec((1, DIFF_QK), full),
                  pl.BlockSpec((tm, DIFF_QK), row), pl.BlockSpec((tm, DIFF_QK), row)],
        out_specs=[pl.BlockSpec((tm, w), row), pl.BlockSpec((tm, w), row)],
        out_shape=[jax.ShapeDtypeStruct((n_rows, w), BF16)] * 2,
        compiler_params=_cparams(("parallel",)),
        name="diff_prep",
    )(proj, proj, g_q.reshape(1, DIFF_QK), g_k.reshape(1, DIFF_QK), cos, sin)


def _scores(q, k):
    return lax.dot_general(q, k, (((1,), (1,)), ((), ())), preferred_element_type=F32)


def _online_update(s, v, m_ref, l_ref, acc_ref, c):
    tiles = [s[:, t * LANES:(t + 1) * LANES] for t in range(s.shape[1] // LANES)]
    m_old = m_ref[c]
    m_new = jnp.maximum(m_old, jnp.max(functools.reduce(jnp.maximum, tiles), axis=1, keepdims=True))
    alpha = jnp.exp2(m_old - m_new)
    m_ref[c] = m_new
    ps = [jnp.exp2(t - m_new) for t in tiles]
    l_ref[c] = alpha * l_ref[c] + functools.reduce(jnp.add, ps)
    pv = jnp.dot(jnp.concatenate(ps, axis=1).astype(BF16), v, preferred_element_type=F32)
    reps = acc_ref.shape[-1] // LANES
    acc_ref[c] = acc_ref[c] * jnp.concatenate([alpha] * reps, axis=1) + pv


def _attn_kernel(*refs, n_comp, n_lat, tk, finalize):
    q_ref, kc_ref, vc_ref = refs[:3]
    n_in = 3
    if n_lat:
        kl_ref, vl_ref = refs[3:5]
        n_in = 5
    extras = refs[n_in:-5]
    o_ref, s_ref, m_ref, l_ref, acc_ref = refs[-5:]
    dk = q_ref.shape[1] // n_comp
    m_ref[...] = jnp.full(m_ref.shape, -jnp.inf, F32)
    l_ref[...] = jnp.zeros(l_ref.shape, F32)
    acc_ref[...] = jnp.zeros(acc_ref.shape, F32)

    def chunk_scores(k, c):
        return _scores(q_ref[:, c * dk:(c + 1) * dk], k[:, c * dk:(c + 1) * dk])

    def put_scores(slot, j):
        k = kl_ref[pl.ds(pl.multiple_of(j * tk, tk), tk), :]
        for c in range(n_comp):
            s_ref[slot, c] = chunk_scores(k, c)

    def consume(slot, j):
        v = vl_ref[pl.ds(pl.multiple_of(j * tk, tk), tk), :]
        for c in range(n_comp):
            _online_update(s_ref[slot, c], v, m_ref, l_ref, acc_ref, c)

    kc = kc_ref[...]
    s_ctx = [chunk_scores(kc, c) for c in range(n_comp)]
    if n_lat:
        put_scores(0, 0)
    for c in range(n_comp):
        _online_update(s_ctx[c], vc_ref[...], m_ref, l_ref, acc_ref, c)
    if n_lat:
        def pair(i, last):
            put_scores(1, 2 * i + 1)
            consume(0, 2 * i)
            if not last:
                put_scores(0, 2 * i + 2)
            consume(1, 2 * i + 1)

        def body(i, carry):
            pair(i, False)
            return carry

        lax.fori_loop(0, n_lat // 2 - 1, body, 0)
        pair(n_lat // 2 - 1, True)
    outs = [acc_ref[c] / jnp.sum(l_ref[c], axis=1, keepdims=True) for c in range(n_comp)]
    o_ref[...] = finalize(outs, *extras).astype(o_ref.dtype)


def _mla_finalize(outs):
    return outs[0]


def _diff_finalize(outs, lam_ref, g_ref, *, lam_init):
    lv = lam_ref[...]
    lam = (jnp.exp(jnp.sum(lv[0:1, :] * lv[1:2, :], axis=1, keepdims=True))
           - jnp.exp(jnp.sum(lv[2:3, :] * lv[3:4, :], axis=1, keepdims=True)) + lam_init)
    o = outs[0] - lam * outs[1]
    return o * lax.rsqrt(jnp.mean(o * o, axis=-1, keepdims=True) + EPS) * g_ref[...] * (1.0 - lam_init)


def _attention(finalize, n_comp, q, k, v, extra, heads, qw, vw, v_col0, batch, seq, ctx_len, latent, tq, tk):
    lat_ctx_blocks = batch * seq // ctx_len
    ctx_spec = lambda width, c0: pl.BlockSpec((ctx_len, width), lambda b, h, i: (lat_ctx_blocks + b, c0 + h))
    if latent:
        q_spec = pl.BlockSpec((tq, qw), lambda b, h, i: (b * (seq // tq) + i, h))
        in_specs = [q_spec, ctx_spec(qw, 0), ctx_spec(vw, v_col0),
                    pl.BlockSpec((seq, qw), lambda b, h, i: (b, h)),
                    pl.BlockSpec((seq, vw), lambda b, h, i: (b, v_col0 + h))]
        args = [q, k, v, k, v]
        grid = (batch, heads, seq // tq)
        out_spec = pl.BlockSpec((tq, vw), lambda b, h, i: (b * (seq // tq) + i, h))
        out_rows = batch * seq
        n_lat = seq // tk
        assert n_lat >= 2 and n_lat % 2 == 0
    else:
        tq = tk = ctx_len
        in_specs = [ctx_spec(qw, 0), ctx_spec(qw, 0), ctx_spec(vw, v_col0)]
        args = [q, k, v]
        grid = (batch, heads, 1)
        out_spec = pl.BlockSpec((ctx_len, vw), lambda b, h, i: (b, h))
        out_rows = batch * ctx_len
        n_lat = 0
    for e in extra:
        in_specs.append(pl.BlockSpec(e.shape, lambda b, h, i: (0, 0)))
        args.append(e)
    return pl.pallas_call(
        functools.partial(_attn_kernel, n_comp=n_comp, n_lat=n_lat, tk=tk, finalize=finalize),
        grid=grid,
        in_specs=in_specs,
        out_specs=out_spec,
        out_shape=jax.ShapeDtypeStruct((out_rows, heads * vw), BF16),
        scratch_shapes=[pltpu.VMEM((2, n_comp, tq, tk), F32),
                        pltpu.VMEM((n_comp, tq, LANES), F32),
                        pltpu.VMEM((n_comp, tq, LANES), F32),
                        pltpu.VMEM((n_comp, tq, vw), F32)],
        compiler_params=_cparams(("parallel", "parallel", "arbitrary")),
        name="attn_latent" if latent else "attn_ctx",
    )(*args)


def _rope_tables(rows, rot_dim):
    axis_dim = rot_dim // 2
    inv = ROPE_THETA ** (-jnp.arange(0, axis_dim, 2, dtype=F32) / axis_dim)
    row = jnp.repeat(jnp.arange(rows, dtype=F32), GRID_W)
    colv = (jnp.arange(rows * GRID_W) % GRID_W).astype(F32)
    ar, ac = row[:, None] * inv, colv[:, None] * inv
    cos = jnp.concatenate([jnp.cos(ar), jnp.cos(ar), jnp.cos(ac), jnp.cos(ac)], axis=1)
    sin = jnp.concatenate([-jnp.sin(ar), jnp.sin(ar), -jnp.sin(ac), jnp.sin(ac)], axis=1)
    return cos, sin


def _token_tables(cos, sin, batch, ctx_rows, left, right):
    n = cos.shape[0]
    cos = jnp.concatenate([jnp.ones((n, left), F32), cos, jnp.ones((n, right), F32)], axis=1)
    sin = jnp.concatenate([jnp.zeros((n, left), F32), sin, jnp.zeros((n, right), F32)], axis=1)
    w = cos.shape[1]
    cos = jnp.concatenate([jnp.tile(cos, (batch, 1)), jnp.ones((ctx_rows, w), F32)], axis=0)
    sin = jnp.concatenate([jnp.tile(sin, (batch, 1)), jnp.zeros((ctx_rows, w), F32)], axis=0)
    return cos, sin


def _layout_w_in(w):
    d = w.shape[0]
    s1, s2 = SSD_COLS, SSD_COLS + MLA_COLS
    xbc0 = SSD_INNER
    z = w[:, :SSD_INNER]
    xs = w[:, xbc0:xbc0 + SSD_INNER]
    bm = w[:, xbc0 + SSD_INNER:xbc0 + SSD_INNER + SSD_GN]
    cm = w[:, xbc0 + SSD_INNER + SSD_GN:xbc0 + SSD_CONV_CH]
    dt = w[:, xbc0 + SSD_CONV_CH:s1]
    cq = w[:, s1:s1 + MLA_Q_LORA]
    ckv = w[:, s1 + MLA_Q_LORA:s1 + MLA_Q_LORA + MLA_KV_LORA]
    kr = w[:, s1 + MLA_Q_LORA + MLA_KV_LORA:s2]
    qk = 2 * DIFF_HEADS * DIFF_QK
    dq, dk, dv = w[:, s2:s2 + qk], w[:, s2 + qk:s2 + 2 * qk], w[:, s2 + 2 * qk:]
    zeros = lambda n: jnp.zeros((d, n), w.dtype)
    main = jnp.concatenate([z, xs, dq, dk, dv, cq, bm, cm, ckv, kr, zeros(LANES - MLA_ROPE), zeros(LANES)], axis=1)
    dtw = jnp.concatenate([dt, zeros(LANES - 2 * SSD_HEADS)], axis=1)
    return main.astype(BF16), dtw.astype(BF16)


def _layout_mla(w_uq, w_ukv, g_q, g_k):
    pad = MLA_QK_PAD - MLA_QK
    uq = jnp.pad(w_uq.reshape(MLA_Q_LORA, MLA_HEADS, MLA_QK), ((0, 0), (0, 0), (0, pad)))
    ukv = w_ukv.reshape(MLA_KV_LORA, MLA_HEADS, MLA_NOPE + MLA_V)
    return dict(
        w_uq=uq.reshape(MLA_Q_LORA, MLA_HEADS * MLA_QK_PAD).astype(BF16),
        w_ukn=ukv[:, :, :MLA_NOPE].reshape(MLA_KV_LORA, MLA_HEADS * MLA_NOPE).astype(BF16),
        w_uv=ukv[:, :, MLA_NOPE:].reshape(MLA_KV_LORA, MLA_HEADS * MLA_V).astype(BF16),
        g_q=jnp.pad(g_q, (0, pad)).reshape(1, MLA_QK_PAD),
        g_k=jnp.pad(g_k, (0, pad)).reshape(1, MLA_QK_PAD),
    )


def kernel(x, c, ctx, c_ctx, w_mod, b_mod, g_norm, w_ffn_gate, w_ffn_up, w_ffn_down, w_in, ssd_conv_w, ssd_conv_b, ssd_dt_bias, ssd_a_log, ssd_d, ssd_norm_g, mla_g_cq, mla_w_uq, mla_g_ckv, mla_w_ukv, mla_g_q, mla_g_k, diff_g_q, diff_g_k, diff_lambda, diff_subln_g, w_out):
    batch, seq, d = x.shape
    ctx_len = ctx.shape[1]
    depth = w_mod.shape[0]
    d_ff = w_ffn_gate.shape[-1]
    lat_rows, ctx_rows = batch * seq, batch * ctx_len
    n_rows = lat_rows + ctx_rows
    assert w_in.shape[-1] == IN_COLS and w_out.shape[1] == SSD_INNER + MLA_HEADS * MLA_V + DIFF_HEADS * DIFF_V
    assert seq % GRID_W == 0 and batch < MOD_ROWS
    tm = _pick(math.gcd(seq, ctx_rows), (512, 256, 128))
    tr = _pick(math.gcd(seq, ctx_len), (256, 128))
    assert seq % tm == 0 and ctx_rows % tm == 0 and seq % tr == 0 and ctx_len % tr == 0
    assert ctx_len % SSD_CHUNK == 0 and seq % SSD_CHUNK == 0 and lat_rows % ctx_len == 0
    tq = _pick(seq, (512, 256, 128))
    tk = _pick(seq // 2, (512, 256, 128))

    def group_of_row(r):
        return jnp.minimum(r // seq, batch)

    xs = jnp.concatenate([x.reshape(lat_rows, d), ctx.reshape(ctx_rows, d)], axis=0)
    c_rows = jnp.concatenate([c, c_ctx[None, :], jnp.zeros((MOD_ROWS - batch - 1, d), F32)], axis=0)
    mod = _modulation(c_rows, w_mod, b_mod).reshape(depth, MOD_ROWS, 1, N_MOD * d)

    rows = seq // GRID_W
    cos_m, sin_m = _token_tables(*_rope_tables(rows, MLA_ROPE), batch, ctx_rows, MLA_NOPE,
                                 MLA_QK_PAD - MLA_QK)
    cos_d, sin_d = _token_tables(*_rope_tables(rows, DIFF_QK), batch, ctx_rows, 0, 0)

    tn_ff = _pick(d_ff, (512, 256, 128))
    tn_d = _pick(d, (512, 256, 128))
    tn_in = _pick(PROJ_COLS, (768,))

    def ffn(xs, li, j, norm_idx, gate_idx, m_rows):
        h = _norm_mod(xs, g_norm[li, norm_idx], mod[li], norm_idx, m_rows, group_of_row, tr)
        act = _matmul_swiglu(h, w_ffn_gate[li, j].astype(BF16), w_ffn_up[li, j].astype(BF16), m_rows, tm, tn_ff)
        return _matmul_resid([(act, w_ffn_down[li, j].astype(BF16))], xs, mod[li], gate_idx, 0.5, m_rows,
                             group_of_row, tm, tn_d)

    for li in range(depth):
        need_ctx = li < depth - 1
        out_rows = n_rows if need_ctx else lat_rows
        lam_init = 0.8 - 0.6 * math.exp(-0.3 * li)

        xs = ffn(xs, li, 0, 0, 2, n_rows)

        h = _norm_mod(xs, g_norm[li, 1], mod[li], 1, n_rows, group_of_row, tr)
        w_main, w_dt = _layout_w_in(w_in[li])
        proj = _matmul(h, w_main, n_rows, BF16, tm, tn_in)
        dt_raw = _matmul(h, w_dt, n_rows, F32, tm, LANES)

        xbc = _ssd_conv(proj, ssd_conv_w[li], ssd_conv_b[li], n_rows, lat_rows, seq, ctx_len, tr)
        yf, yb = _ssd_scan(xbc, dt_raw, ssd_dt_bias[li], ssd_a_log[li], batch, seq, ctx_len)
        ssd_o = _ssd_out(yf, yb, xbc, proj, ssd_d[li], ssd_norm_g[li], out_rows, tr)

        mla_p = _layout_mla(mla_w_uq[li], mla_w_ukv[li], mla_g_q[li], mla_g_k[li])
        mla_p['g_cq'] = mla_g_cq[li].reshape(1, MLA_Q_LORA)
        mla_p['g_ckv'] = mla_g_ckv[li].reshape(1, MLA_KV_LORA)
        mq, mk, mv = _mla_prep(proj, mla_p, cos_m, sin_m, n_rows, tr)
        mla_args = (_mla_finalize, 1, mq, mk, mv, [], MLA_HEADS, MLA_QK_PAD, MLA_V, 0, batch, seq, ctx_len)
        mla_o = _attention(*mla_args, True, tq, tk)

        dq, dk = _diff_prep(proj, diff_g_q[li], diff_g_k[li], cos_d, sin_d, n_rows, tr)
        diff_fin = functools.partial(_diff_finalize, lam_init=lam_init)
        extra = [diff_lambda[li], diff_subln_g[li].reshape(1, DIFF_V)]
        diff_args = (diff_fin, 2, dq, dk, proj, extra, DIFF_HEADS, 2 * DIFF_QK, DIFF_V, PC_DV // DIFF_V,
                     batch, seq, ctx_len)
        diff_o = _attention(*diff_args, True, tq, tk)
        if need_ctx:
            mla_o = jnp.concatenate([mla_o, _attention(*mla_args, False, tq, tk)], axis=0)
            diff_o = jnp.concatenate([diff_o, _attention(*diff_args, False, tq, tk)], axis=0)

        wo = w_out[li].astype(BF16)
        o1, o2 = SSD_INNER, SSD_INNER + MLA_HEADS * MLA_V
        xs = _matmul_resid([(ssd_o, wo[:o1]), (mla_o, wo[o1:o2]), (diff_o, wo[o2:])], xs, mod[li], 5, 1.0,
                           out_rows, group_of_row, tm, tn_d)

        xs = ffn(xs, li, 1, 2, 8, out_rows)

    return xs[:lat_rows].reshape(batch, seq, d)
```

```python
import functools
import math

import jax
import jax.numpy as jnp
from jax import lax
from jax.experimental import pallas as pl
from jax.experimental.pallas import tpu as pltpu

F32 = jnp.float32
BF16 = jnp.bfloat16

GRID_W = 64
EPS = 1e-6
ROPE_THETA = 10000.0
N_MOD = 9
SSD_INNER = 2048
SSD_HEAD_DIM = 64
SSD_HEADS = 32
SSD_GROUPS = 4
SSD_STATE = 128
SSD_CONV = 5
SSD_CHUNK = 128
SSD_GN = SSD_GROUPS * SSD_STATE
SSD_CONV_CH = SSD_INNER + 2 * SSD_GN
SSD_COLS = 2 * SSD_INNER + 2 * SSD_GN + 2 * SSD_HEADS
MLA_HEADS = 8
MLA_NOPE = 128
MLA_ROPE = 64
MLA_V = 128
MLA_Q_LORA = 1024
MLA_KV_LORA = 512
MLA_QK = MLA_NOPE + MLA_ROPE
MLA_QK_PAD = 256
MLA_SCALE = 1.0 / math.sqrt(MLA_QK)
LOG2E = math.log2(math.e)
MLA_COLS = MLA_Q_LORA + MLA_KV_LORA + MLA_ROPE
DIFF_HEADS = 4
DIFF_QK = 128
DIFF_V = 2 * DIFF_QK
DIFF_SCALE = 1.0 / math.sqrt(DIFF_QK)
DIFF_COLS = DIFF_HEADS * (4 * DIFF_QK + DIFF_V)
IN_COLS = SSD_COLS + MLA_COLS + DIFF_COLS

LANES = 128
BF16_SUBLANES = 16
MOD_ROWS = 8
VMEM_LIMIT = 50 * 1024 * 1024

PC_Z = 0
PC_XS = 2048
PC_DQ = 4096
PC_DK = 5120
PC_DV = 6144
PC_CQ = 7168
PC_B = 8192
PC_C = 8704
PC_CKV = 9216
PC_KR = 9728
PC_PAD = 9856
PROJ_COLS = 9984


def _cparams(sem):
    return pltpu.CompilerParams(dimension_semantics=sem, vmem_limit_bytes=VMEM_LIMIT)


def _pick(n, prefs):
    for p in prefs:
        if n % p == 0:
            return p
    return n


def _silu(x):
    return x / (1.0 + jnp.exp(-x))


def _weight_spec(w, lead, tn):
    return pl.BlockSpec((None,) * len(lead) + (w.shape[-2], tn), lambda j, i: tuple(lead) + (0, j))


def _mm_kernel(a_ref, b_ref, o_ref):
    o_ref[...] = jnp.dot(a_ref[...], b_ref[...], preferred_element_type=F32).astype(o_ref.dtype)


def _matmul(a, b, m_rows, out_dtype, tm, tn):
    k = a.shape[1]
    n = b.shape[1]
    return pl.pallas_call(
        _mm_kernel,
        grid=(n // tn, m_rows // tm),
        in_specs=[pl.BlockSpec((tm, k), lambda j, i: (i, 0)),
                  pl.BlockSpec((k, tn), lambda j, i: (0, j))],
        out_specs=pl.BlockSpec((tm, tn), lambda j, i: (i, j)),
        out_shape=jax.ShapeDtypeStruct((m_rows, n), out_dtype),
        compiler_params=_cparams(("parallel", "parallel")),
        name="mm_plain",
    )(a, b)


def _swiglu_kernel(a_ref, wg_ref, wu_ref, o_ref):
    a = a_ref[...]
    g = jnp.dot(a, wg_ref[...], preferred_element_type=F32)
    u = jnp.dot(a, wu_ref[...], preferred_element_type=F32)
    o_ref[...] = (_silu(g) * u).astype(o_ref.dtype)


def _matmul_swiglu(a, wg, wu, lead, m_rows, tm, tn):
    k = a.shape[1]
    n = wg.shape[-1]
    return pl.pallas_call(
        _swiglu_kernel,
        grid=(n // tn, m_rows // tm),
        in_specs=[pl.BlockSpec((tm, k), lambda j, i: (i, 0)),
                  _weight_spec(wg, lead, tn), _weight_spec(wu, lead, tn)],
        out_specs=pl.BlockSpec((tm, tn), lambda j, i: (i, j)),
        out_shape=jax.ShapeDtypeStruct((m_rows, n), BF16),
        compiler_params=_cparams(("parallel", "parallel")),
        name="mm_swiglu",
    )(a, wg, wu)


def _resid_kernel(*refs, n_pairs, scale):
    x_ref, gate_ref, o_ref = refs[2 * n_pairs], refs[2 * n_pairs + 1], refs[2 * n_pairs + 2]
    acc = jnp.dot(refs[0][...], refs[1][...], preferred_element_type=F32)
    for p in range(1, n_pairs):
        acc = acc + jnp.dot(refs[2 * p][...], refs[2 * p + 1][...], preferred_element_type=F32)
    o_ref[...] = x_ref[...] + (scale * gate_ref[...]) * acc


def _matmul_resid(pairs, x, mod4, gate_idx, scale, m_rows, group_of_row, tm, tn):
    d = x.shape[1]
    in_specs = []
    args = []
    for a, b, lead in pairs:
        in_specs.append(pl.BlockSpec((tm, a.shape[1]), lambda j, i: (i, 0)))
        in_specs.append(_weight_spec(b, lead, tn))
        args += [a, b]
    in_specs.append(pl.BlockSpec((tm, tn), lambda j, i: (i, j)))
    gate_blk = gate_idx * (d // tn)
    in_specs.append(pl.BlockSpec((None, 1, tn), lambda j, i: (group_of_row(i * tm), 0, gate_blk + j)))
    args += [x, mod4]
    return pl.pallas_call(
        functools.partial(_resid_kernel, n_pairs=len(pairs), scale=scale),
        grid=(d // tn, m_rows // tm),
        in_specs=in_specs,
        out_specs=pl.BlockSpec((tm, tn), lambda j, i: (i, j)),
        out_shape=jax.ShapeDtypeStruct((m_rows, d), F32),
        compiler_params=_cparams(("parallel", "parallel")),
        name="mm_resid",
    )(*args)


def _mod_kernel(c_ref, w_ref, b_ref, o_ref):
    s = _silu(c_ref[...]).astype(BF16)
    w = w_ref[...].astype(BF16)
    o_ref[...] = jnp.dot(s, w, preferred_element_type=F32) + b_ref[...]


def _modulation(c_rows, w_mod, b_mod):
    depth, d, n = w_mod.shape
    tn = _pick(n, (512, 256, 128))
    return pl.pallas_call(
        _mod_kernel,
        grid=(depth, n // tn),
        in_specs=[pl.BlockSpec((MOD_ROWS, d), lambda l, j: (0, 0)),
                  pl.BlockSpec((None, d, tn), lambda l, j: (l, 0, j)),
                  pl.BlockSpec((None, 1, tn), lambda l, j: (l, 0, j))],
        out_specs=pl.BlockSpec((None, MOD_ROWS, tn), lambda l, j: (l, 0, j)),
        out_shape=jax.ShapeDtypeStruct((depth, MOD_ROWS, n), F32),
        compiler_params=_cparams(("parallel", "parallel")),
        name="modulation",
    )(c_rows, w_mod, b_mod.reshape(depth, 1, n))


def _norm_mod_kernel(x_ref, g_ref, shift_ref, scale_ref, o_ref):
    x = x_ref[...]
    y = x * lax.rsqrt(jnp.mean(x * x, axis=-1, keepdims=True) + EPS) * g_ref[...]
    o_ref[...] = (y * (1.0 + scale_ref[...]) + shift_ref[...]).astype(o_ref.dtype)


def _norm_mod(x, g, mod4, idx, m_rows, group_of_row, tm):
    d = x.shape[1]
    return pl.pallas_call(
        _norm_mod_kernel,
        grid=(m_rows // tm,),
        in_specs=[pl.BlockSpec((tm, d), lambda i: (i, 0)),
                  pl.BlockSpec((1, d), lambda i: (0, 0)),
                  pl.BlockSpec((None, 1, d), lambda i: (group_of_row(i * tm), 0, 3 * idx)),
                  pl.BlockSpec((None, 1, d), lambda i: (group_of_row(i * tm), 0, 3 * idx + 1))],
        out_specs=pl.BlockSpec((tm, d), lambda i: (i, 0)),
        out_shape=jax.ShapeDtypeStruct((m_rows, d), BF16),
        compiler_params=_cparams(("parallel",)),
        name="norm_mod",
    )(x, g.reshape(1, d), mod4, mod4)


def _conv_kernel(prev_ref, cur_ref, next_ref, w_ref, b_ref, o_ref, ext_ref, *, tr, lat_rows, seq, ctx_len):
    r0 = pl.program_id(0) * tr
    in_lat = r0 < lat_rows
    off = jnp.where(in_lat, r0 % seq, (r0 - lat_rows) % ctx_len)
    seg = jnp.where(in_lat, seq, ctx_len)
    first = off == 0
    last = off + tr == seg
    h = BF16_SUBLANES
    ext_ref[0:h, :] = jnp.where(first, 0.0, prev_ref[...].astype(F32))
    ext_ref[h:h + tr, :] = cur_ref[...].astype(F32)
    ext_ref[h + tr:2 * h + tr, :] = jnp.where(last, 0.0, next_ref[...].astype(F32))
    acc = b_ref[...] + w_ref[0:1, :] * ext_ref[pl.ds(h - 2, tr), :]
    for k in range(1, SSD_CONV):
        acc = acc + w_ref[k:k + 1, :] * ext_ref[pl.ds(h - 2 + k, tr), :]
    o_ref[...] = _silu(acc).astype(o_ref.dtype)


def _ssd_conv(proj, conv_w, conv_b, n_rows, lat_rows, seq, ctx_len, tr):
    tc = 512
    h = BF16_SUBLANES
    rb = tr // h
    n_halo = n_rows // h
    xs_blk, bc_blk = PC_XS // tc, PC_B // tc
    n_xs = SSD_INNER // tc

    def col(j):
        return jnp.where(j < n_xs, xs_blk + j, bc_blk + (j - n_xs))

    return pl.pallas_call(
        functools.partial(_conv_kernel, tr=tr, lat_rows=lat_rows, seq=seq, ctx_len=ctx_len),
        grid=(n_rows // tr, SSD_CONV_CH // tc),
        in_specs=[pl.BlockSpec((h, tc), lambda i, j: (jnp.maximum(i * rb - 1, 0), col(j))),
                  pl.BlockSpec((tr, tc), lambda i, j: (i, col(j))),
                  pl.BlockSpec((h, tc), lambda i, j: (jnp.minimum((i + 1) * rb, n_halo - 1), col(j))),
                  pl.BlockSpec((SSD_CONV, tc), lambda i, j: (0, j)),
                  pl.BlockSpec((1, tc), lambda i, j: (0, j))],
        out_specs=pl.BlockSpec((tr, tc), lambda i, j: (i, j)),
        out_shape=jax.ShapeDtypeStruct((n_rows, SSD_CONV_CH), BF16),
        scratch_shapes=[pltpu.VMEM((tr + 2 * h, tc), F32)],
        compiler_params=_cparams(("parallel", "parallel")),
        name="ssd_conv",
    )(proj, proj, proj, conv_w, conv_b.reshape(1, SSD_CONV_CH))


N_PAIRS = SSD_HEADS // 2
PAIRS_PER_GROUP = N_PAIRS // SSD_GROUPS


def _ssd_direction(d, x_ref, b_ref, c_ref, dt_ref, bias_ref, alog_ref, y_ref, st_ref,
                   cs_ref, cst_ref, dtt_ref, woutt_ref, g_ref, bt_ref, yoff_ref, xp_ref, yp_ref):
    q = SSD_CHUNK
    lane = lax.broadcasted_iota(jnp.int32, (q, LANES), 1)
    row = lax.broadcasted_iota(jnp.int32, (q, q), 0)
    col = lax.broadcasted_iota(jnp.int32, (q, q), 1)
    mask = (row >= col) if d == 0 else (row <= col)

    z = dt_ref[...] + bias_ref[...]
    dt = jnp.maximum(z, 0.0) + jnp.log1p(jnp.exp(-jnp.abs(z)))
    a_dt = dt * (-jnp.exp(alog_ref[...]))
    hi = a_dt.astype(BF16)
    r1 = a_dt - hi.astype(F32)
    mid = r1.astype(BF16)
    lo = (r1 - mid.astype(F32)).astype(BF16)
    tri = jnp.where(mask, 1.0, 0.0).astype(BF16)
    c3 = jnp.dot(tri, jnp.concatenate([hi, mid, lo], axis=1), preferred_element_type=F32)
    cs = c3[:, 0:LANES] + c3[:, LANES:2 * LANES] + c3[:, 2 * LANES:3 * LANES]
    end = q - 1 if d == 0 else 0
    a_tot = cs[end:end + 1, :]
    cs_ref[...] = cs
    cst_ref[...] = cs.T
    dtt_ref[...] = dt.T
    woutt_ref[...] = (dt * jnp.exp(a_tot - cs)).T

    for g in range(SSD_GROUPS):
        bg = b_ref[:, g * SSD_STATE:(g + 1) * SSD_STATE]
        cg = c_ref[:, g * SSD_STATE:(g + 1) * SSD_STATE]
        g_ref[g] = lax.dot_general(cg, bg, (((1,), (1,)), ((), ())), preferred_element_type=F32)
        bt_ref[g] = bg.astype(F32).T
        for pp in range(PAIRS_PER_GROUP):
            p = g * PAIRS_PER_GROUP + pp
            yoff_ref[p] = jnp.dot(cg, st_ref[p].astype(BF16), preferred_element_type=F32)
    for p in range(N_PAIRS):
        xp_ref[p] = x_ref[:, p * LANES:(p + 1) * LANES]

    half = lane < SSD_HEAD_DIM

    def pair_body(p, carry):
        g = p // PAIRS_PER_GROUP
        h0 = 2 * p + SSD_HEADS * d
        gmat = g_ref[g]
        bt = bt_ref[g]
        xp = xp_ref[p]
        ys, ups, dins = [], [], []
        for hh in (h0, h0 + 1):
            cs_col = jnp.sum(jnp.where(lane == hh, cs_ref[...], 0.0), axis=1, keepdims=True)
            lmat = jnp.exp(jnp.where(mask, cs_col - cst_ref[pl.ds(hh, 1), :], -1e30))
            m = (gmat * lmat * dtt_ref[pl.ds(hh, 1), :]).astype(BF16)
            ys.append(jnp.dot(m, xp, preferred_element_type=F32))
            bw = (bt * woutt_ref[pl.ds(hh, 1), :]).astype(BF16)
            ups.append(jnp.dot(bw, xp, preferred_element_type=F32))
            dins.append(jnp.exp(cs_col))
        din = jnp.where(half, dins[0], dins[1])
        yp_ref[p] = jnp.where(half, ys[0], ys[1]) + yoff_ref[p] * din
        st_ref[p] = st_ref[p] * din[end:end + 1, :] + jnp.where(half, ups[0], ups[1])
        return carry

    lax.fori_loop(0, N_PAIRS, pair_body, 0, unroll=4)
    for p in range(N_PAIRS):
        y_ref[:, p * LANES:(p + 1) * LANES] = yp_ref[p]


def _ssd_kernel(xf_ref, bf_ref, cf_ref, dtf_ref, xb_ref, bb_ref, cb_ref, dtb_ref, bias_ref, alog_ref,
                yf_ref, yb_ref, stf_ref, stb_ref, cs_ref, cst_ref, dtt_ref, woutt_ref,
                g_ref, bt_ref, yoff_ref, xp_ref, yp_ref):
    @pl.when(pl.program_id(1) == 0)
    def _():
        stf_ref[...] = jnp.zeros_like(stf_ref)
        stb_ref[...] = jnp.zeros_like(stb_ref)

    shared = (cs_ref, cst_ref, dtt_ref, woutt_ref, g_ref, bt_ref, yoff_ref, xp_ref, yp_ref)
    _ssd_direction(0, xf_ref, bf_ref, cf_ref, dtf_ref, bias_ref, alog_ref, yf_ref, stf_ref, *shared)
    _ssd_direction(1, xb_ref, bb_ref, cb_ref, dtb_ref, bias_ref, alog_ref, yb_ref, stb_ref, *shared)


def _ssd_scan(xbc, dt_raw, dt_bias, a_log, batch, seq, ctx_len):
    q = SSD_CHUNK
    n_rows = xbc.shape[0]
    ncc, nlc = ctx_len // q, seq // q
    lat_blocks = batch * seq // q

    def fwd_blk(b, s):
        return jnp.where(s < ncc, lat_blocks + b * ncc + s, b * nlc + (s - ncc))

    def bwd_blk(b, s):
        return jnp.where(s < ncc, lat_blocks + b * ncc + (ncc - 1 - s), b * nlc + (nlc - 1 - (s - ncc)))

    def specs(blk):
        return [pl.BlockSpec((q, SSD_INNER), lambda b, s: (blk(b, s), 0)),
                pl.BlockSpec((q, SSD_GN), lambda b, s: (blk(b, s), SSD_INNER // SSD_GN)),
                pl.BlockSpec((q, SSD_GN), lambda b, s: (blk(b, s), SSD_INNER // SSD_GN + 1)),
                pl.BlockSpec((q, LANES), lambda b, s: (blk(b, s), 0))]

    pad = jnp.zeros((LANES - 2 * SSD_HEADS,), F32)
    bias = jnp.concatenate([dt_bias.reshape(-1), pad]).reshape(1, LANES)
    alog = jnp.concatenate([a_log.reshape(-1), pad]).reshape(1, LANES)
    tile = (q, LANES)
    return pl.pallas_call(
        _ssd_kernel,
        grid=(batch, ncc + nlc),
        in_specs=specs(fwd_blk) + specs(bwd_blk) + [pl.BlockSpec((1, LANES), lambda b, s: (0, 0))] * 2,
        out_specs=[pl.BlockSpec((q, SSD_INNER), lambda b, s: (fwd_blk(b, s), 0)),
                   pl.BlockSpec((q, SSD_INNER), lambda b, s: (bwd_blk(b, s), 0))],
        out_shape=[jax.ShapeDtypeStruct((n_rows, SSD_INNER), F32)] * 2,
        scratch_shapes=[pltpu.VMEM((N_PAIRS, SSD_STATE, LANES), F32),
                        pltpu.VMEM((N_PAIRS, SSD_STATE, LANES), F32),
                        pltpu.VMEM(tile, F32), pltpu.VMEM(tile, F32), pltpu.VMEM(tile, F32),
                        pltpu.VMEM(tile, F32),
                        pltpu.VMEM((SSD_GROUPS, q, q), F32),
                        pltpu.VMEM((SSD_GROUPS, SSD_STATE, q), F32),
                        pltpu.VMEM((N_PAIRS, q, LANES), F32),
                        pltpu.VMEM((N_PAIRS, q, LANES), BF16),
                        pltpu.VMEM((N_PAIRS, q, LANES), F32)],
        compiler_params=_cparams(("parallel", "arbitrary")),
        name="ssd_scan",
    )(xbc, xbc, xbc, dt_raw, xbc, xbc, xbc, dt_raw, bias, alog)


def _ssd_out_kernel(yf_ref, yb_ref, x_ref, z_ref, d_ref, g_ref, o_ref):
    z = z_ref[...].astype(F32)
    y = (yf_ref[...] + yb_ref[...] + d_ref[...] * x_ref[...].astype(F32)) * _silu(z)
    y = y * lax.rsqrt(jnp.mean(y * y, axis=-1, keepdims=True) + EPS) * g_ref[...]
    o_ref[...] = y.astype(o_ref.dtype)


def _ssd_out(yf, yb, xbc, proj, ssd_d, norm_g, n_rows, tm):
    w = SSD_INNER
    row = lambda i: (i, 0)
    return pl.pallas_call(
        _ssd_out_kernel,
        grid=(n_rows // tm,),
        in_specs=[pl.BlockSpec((tm, w), row), pl.BlockSpec((tm, w), row), pl.BlockSpec((tm, w), row),
                  pl.BlockSpec((tm, w), lambda i: (i, PC_Z // w)),
                  pl.BlockSpec((1, w), lambda i: (0, 0)), pl.BlockSpec((1, w), lambda i: (0, 0))],
        out_specs=pl.BlockSpec((tm, w), row),
        out_shape=jax.ShapeDtypeStruct((n_rows, w), BF16),
        compiler_params=_cparams(("parallel",)),
        name="ssd_out",
    )(yf, yb, xbc, proj, jnp.repeat(ssd_d, SSD_HEAD_DIM).reshape(1, w), norm_g.reshape(1, w))


def _rms(x, g, n):
    return x * lax.rsqrt(jnp.sum(x * x, axis=-1, keepdims=True) * (1.0 / n) + EPS) * g


def _rope(x, cos, sin_signed, half):
    w = x.shape[1]
    lane = lax.broadcasted_iota(jnp.int32, x.shape, 1)
    partner = jnp.where(lane % (2 * half) < half, pltpu.roll(x, w - half, 1), pltpu.roll(x, half, 1))
    return x * cos + partner * sin_signed


def _mla_prep_kernel(cq_ref, ckv_ref, kr_ref, gcq_ref, wuq_ref, gckv_ref, wukn_ref, wuv_ref, gq_ref, gk_ref,
                     cos_ref, sin_ref, q_ref, k_ref, v_ref):
    hq = _rms(cq_ref[...].astype(F32), gcq_ref[...], MLA_Q_LORA).astype(BF16)
    q_all = jnp.dot(hq, wuq_ref[...], preferred_element_type=F32)
    hkv = _rms(ckv_ref[...].astype(F32), gckv_ref[...], MLA_KV_LORA).astype(BF16)
    kn_all = jnp.dot(hkv, wukn_ref[...], preferred_element_type=F32)
    v_ref[...] = jnp.dot(hkv, wuv_ref[...], preferred_element_type=F32).astype(v_ref.dtype)
    kr = kr_ref[...].astype(F32)
    lane = lax.broadcasted_iota(jnp.int32, kr.shape, 1)
    kr = jnp.where(lane < MLA_ROPE, kr, 0.0)
    cos, sin = cos_ref[...], sin_ref[...]
    w = MLA_QK_PAD
    for h in range(MLA_HEADS):
        qh = _rms(q_all[:, h * w:(h + 1) * w], gq_ref[...], MLA_QK)
        q_ref[:, h * w:(h + 1) * w] = (_rope(qh, cos, sin, MLA_ROPE // 4) * (MLA_SCALE * LOG2E)).astype(q_ref.dtype)
        kh = jnp.concatenate([kn_all[:, h * MLA_NOPE:(h + 1) * MLA_NOPE], kr], axis=1)
        kh = _rms(kh, gk_ref[...], MLA_QK)
        k_ref[:, h * w:(h + 1) * w] = _rope(kh, cos, sin, MLA_ROPE // 4).astype(k_ref.dtype)


def _mla_prep(proj, p, cos, sin, n_rows, tm):
    hw = MLA_HEADS * MLA_QK_PAD
    hv = MLA_HEADS * MLA_V
    full = lambda i: (0, 0)
    return pl.pallas_call(
        _mla_prep_kernel,
        grid=(n_rows // tm,),
        in_specs=[pl.BlockSpec((tm, MLA_Q_LORA), lambda i: (i, PC_CQ // MLA_Q_LORA)),
                  pl.BlockSpec((tm, MLA_KV_LORA), lambda i: (i, PC_CKV // MLA_KV_LORA)),
                  pl.BlockSpec((tm, LANES), lambda i: (i, PC_KR // LANES)),
                  pl.BlockSpec((1, MLA_Q_LORA), full), pl.BlockSpec((MLA_Q_LORA, hw), full),
                  pl.BlockSpec((1, MLA_KV_LORA), full), pl.BlockSpec((MLA_KV_LORA, hv), full),
                  pl.BlockSpec((MLA_KV_LORA, hv), full),
                  pl.BlockSpec((1, MLA_QK_PAD), full), pl.BlockSpec((1, MLA_QK_PAD), full),
                  pl.BlockSpec((tm, MLA_QK_PAD), lambda i: (i, 0)),
                  pl.BlockSpec((tm, MLA_QK_PAD), lambda i: (i, 0))],
        out_specs=[pl.BlockSpec((tm, hw), lambda i: (i, 0)), pl.BlockSpec((tm, hw), lambda i: (i, 0)),
                   pl.BlockSpec((tm, hv), lambda i: (i, 0))],
        out_shape=[jax.ShapeDtypeStruct((n_rows, hw), BF16), jax.ShapeDtypeStruct((n_rows, hw), BF16),
                   jax.ShapeDtypeStruct((n_rows, hv), BF16)],
        compiler_params=_cparams(("parallel",)),
        name="mla_prep",
    )(proj, proj, proj, p['g_cq'], p['w_uq'], p['g_ckv'], p['w_ukn'], p['w_uv'], p['g_q'], p['g_k'], cos, sin)


def _diff_prep_kernel(q_in, k_in, gq_ref, gk_ref, cos_ref, sin_ref, q_ref, k_ref):
    cos, sin = cos_ref[...], sin_ref[...]
    w = DIFF_QK
    for c in range(2 * DIFF_HEADS):
        sl = slice(c * w, (c + 1) * w)
        qc = _rope(_rms(q_in[:, sl].astype(F32), gq_ref[...], w), cos, sin, w // 4)
        q_ref[:, sl] = (qc * (DIFF_SCALE * LOG2E)).astype(q_ref.dtype)
        kc = _rope(_rms(k_in[:, sl].astype(F32), gk_ref[...], w), cos, sin, w // 4)
        k_ref[:, sl] = kc.astype(k_ref.dtype)


def _diff_prep(proj, g_q, g_k, cos, sin, n_rows, tm):
    w = 2 * DIFF_HEADS * DIFF_QK
    full = lambda i: (0, 0)
    row = lambda i: (i, 0)
    return pl.pallas_call(
        _diff_prep_kernel,
        grid=(n_rows // tm,),
        in_specs=[pl.BlockSpec((tm, w), lambda i: (i, PC_DQ // w)),
                  pl.BlockSpec((tm, w), lambda i: (i, PC_DK // w)),
                  pl.BlockSpec((1, DIFF_QK), full), pl.BlockSpec((1, DIFF_QK), full),
                  pl.BlockSpec((tm, DIFF_QK), row), pl.BlockSpec((tm, DIFF_QK), row)],
        out_specs=[pl.BlockSpec((tm, w), row), pl.BlockSpec((tm, w), row)],
        out_shape=[jax.ShapeDtypeStruct((n_rows, w), BF16)] * 2,
        compiler_params=_cparams(("parallel",)),
        name="diff_prep",
    )(proj, proj, g_q.reshape(1, DIFF_QK), g_k.reshape(1, DIFF_QK), cos, sin)


def _scores(q, k):
    return lax.dot_general(q, k, (((1,), (1,)), ((), ())), preferred_element_type=F32)


def _online_update(s, v, m_ref, l_ref, acc_ref, c):
    tiles = [s[:, t * LANES:(t + 1) * LANES] for t in range(s.shape[1] // LANES)]
    m_old = m_ref[c]
    m_new = jnp.maximum(m_old, jnp.max(functools.reduce(jnp.maximum, tiles), axis=1, keepdims=True))
    alpha = jnp.exp2(m_old - m_new)
    m_ref[c] = m_new
    ps = [jnp.exp2(t - m_new) for t in tiles]
    l_ref[c] = alpha * l_ref[c] + functools.reduce(jnp.add, ps)
    pv = jnp.dot(jnp.concatenate(ps, axis=1).astype(BF16), v, preferred_element_type=F32)
    reps = acc_ref.shape[-1] // LANES
    acc_ref[c] = acc_ref[c] * jnp.concatenate([alpha] * reps, axis=1) + pv


def _attn_kernel(*refs, n_comp, n_lat, tk, finalize):
    q_ref, kc_ref, vc_ref = refs[:3]
    n_in = 3
    if n_lat:
        kl_ref, vl_ref = refs[3:5]
        n_in = 5
    extras = refs[n_in:-5]
    o_ref, s_ref, m_ref, l_ref, acc_ref = refs[-5:]
    dk = q_ref.shape[1] // n_comp
    m_ref[...] = jnp.full(m_ref.shape, -jnp.inf, F32)
    l_ref[...] = jnp.zeros(l_ref.shape, F32)
    acc_ref[...] = jnp.zeros(acc_ref.shape, F32)

    def chunk_scores(k, c):
        return _scores(q_ref[:, c * dk:(c + 1) * dk], k[:, c * dk:(c + 1) * dk])

    def put_scores(slot, j):
        k = kl_ref[pl.ds(pl.multiple_of(j * tk, tk), tk), :]
        for c in range(n_comp):
            s_ref[slot, c] = chunk_scores(k, c)

    def consume(slot, j):
        v = vl_ref[pl.ds(pl.multiple_of(j * tk, tk), tk), :]
        for c in range(n_comp):
            _online_update(s_ref[slot, c], v, m_ref, l_ref, acc_ref, c)

    kc = kc_ref[...]
    s_ctx = [chunk_scores(kc, c) for c in range(n_comp)]
    if n_lat:
        put_scores(0, 0)
    for c in range(n_comp):
        _online_update(s_ctx[c], vc_ref[...], m_ref, l_ref, acc_ref, c)
    if n_lat:
        def pair(i, last):
            put_scores(1, 2 * i + 1)
            consume(0, 2 * i)
            if not last:
                put_scores(0, 2 * i + 2)
            consume(1, 2 * i + 1)

        def body(i, carry):
            pair(i, False)
            return carry

        lax.fori_loop(0, n_lat // 2 - 1, body, 0)
        pair(n_lat // 2 - 1, True)
    outs = [acc_ref[c] / jnp.sum(l_ref[c], axis=1, keepdims=True) for c in range(n_comp)]
    o_ref[...] = finalize(outs, *extras).astype(o_ref.dtype)


def _mla_finalize(outs):
    return outs[0]


def _diff_finalize(outs, lam_ref, g_ref, *, lam_init):
    lv = lam_ref[...]
    lam = (jnp.exp(jnp.sum(lv[0:1, :] * lv[1:2, :], axis=1, keepdims=True))
           - jnp.exp(jnp.sum(lv[2:3, :] * lv[3:4, :], axis=1, keepdims=True)) + lam_init)
    o = outs[0] - lam * outs[1]
    return o * lax.rsqrt(jnp.mean(o * o, axis=-1, keepdims=True) + EPS) * g_ref[...] * (1.0 - lam_init)


def _attention(finalize, n_comp, q, k, v, extra, heads, qw, vw, v_col0, batch, seq, ctx_len, latent, tq, tk):
    lat_ctx_blocks = batch * seq // ctx_len
    ctx_spec = lambda width, c0: pl.BlockSpec((ctx_len, width), lambda b, h, i: (lat_ctx_blocks + b, c0 + h))
    if latent:
        q_spec = pl.BlockSpec((tq, qw), lambda b, h, i: (b * (seq // tq) + i, h))
        in_specs = [q_spec, ctx_spec(qw, 0), ctx_spec(vw, v_col0),
                    pl.BlockSpec((seq, qw), lambda b, h, i: (b, h)),
                    pl.BlockSpec((seq, vw), lambda b, h, i: (b, v_col0 + h))]
        args = [q, k, v, k, v]
        grid = (batch, heads, seq // tq)
        out_spec = pl.BlockSpec((tq, vw), lambda b, h, i: (b * (seq // tq) + i, h))
        out_rows = batch * seq
        n_lat = seq // tk
        assert n_lat >= 2 and n_lat % 2 == 0
    else:
        tq = tk = ctx_len
        in_specs = [ctx_spec(qw, 0), ctx_spec(qw, 0), ctx_spec(vw, v_col0)]
        args = [q, k, v]
        grid = (batch, heads, 1)
        out_spec = pl.BlockSpec((ctx_len, vw), lambda b, h, i: (b, h))
        out_rows = batch * ctx_len
        n_lat = 0
    for e in extra:
        in_specs.append(pl.BlockSpec(e.shape, lambda b, h, i: (0, 0)))
        args.append(e)
    return pl.pallas_call(
        functools.partial(_attn_kernel, n_comp=n_comp, n_lat=n_lat, tk=tk, finalize=finalize),
        grid=grid,
        in_specs=in_specs,
        out_specs=out_spec,
        out_shape=jax.ShapeDtypeStruct((out_rows, heads * vw), BF16),
        scratch_shapes=[pltpu.VMEM((2, n_comp, tq, tk), F32),
                        pltpu.VMEM((n_comp, tq, LANES), F32),
                        pltpu.VMEM((n_comp, tq, LANES), F32),
                        pltpu.VMEM((n_comp, tq, vw), F32)],
        compiler_params=_cparams(("parallel", "parallel", "arbitrary")),
        name="attn_latent" if latent else "attn_ctx",
    )(*args)


def _rope_tables(rows, rot_dim):
    axis_dim = rot_dim // 2
    inv = ROPE_THETA ** (-jnp.arange(0, axis_dim, 2, dtype=F32) / axis_dim)
    row = jnp.repeat(jnp.arange(rows, dtype=F32), GRID_W)
    colv = (jnp.arange(rows * GRID_W) % GRID_W).astype(F32)
    ar, ac = row[:, None] * inv, colv[:, None] * inv
    cos = jnp.concatenate([jnp.cos(ar), jnp.cos(ar), jnp.cos(ac), jnp.cos(ac)], axis=1)
    sin = jnp.concatenate([-jnp.sin(ar), jnp.sin(ar), -jnp.sin(ac), jnp.sin(ac)], axis=1)
    return cos, sin


def _token_tables(cos, sin, batch, ctx_rows, left, right):
    n = cos.shape[0]
    cos = jnp.concatenate([jnp.ones((n, left), F32), cos, jnp.ones((n, right), F32)], axis=1)
    sin = jnp.concatenate([jnp.zeros((n, left), F32), sin, jnp.zeros((n, right), F32)], axis=1)
    w = cos.shape[1]
    cos = jnp.concatenate([jnp.tile(cos, (batch, 1)), jnp.ones((ctx_rows, w), F32)], axis=0)
    sin = jnp.concatenate([jnp.tile(sin, (batch, 1)), jnp.zeros((ctx_rows, w), F32)], axis=0)
    return cos, sin


def _layout_w_in(w):
    d = w.shape[0]
    s1, s2 = SSD_COLS, SSD_COLS + MLA_COLS
    xbc0 = SSD_INNER
    z = w[:, :SSD_INNER]
    xs = w[:, xbc0:xbc0 + SSD_INNER]
    bm = w[:, xbc0 + SSD_INNER:xbc0 + SSD_INNER + SSD_GN]
    cm = w[:, xbc0 + SSD_INNER + SSD_GN:xbc0 + SSD_CONV_CH]
    dt = w[:, xbc0 + SSD_CONV_CH:s1]
    cq = w[:, s1:s1 + MLA_Q_LORA]
    ckv = w[:, s1 + MLA_Q_LORA:s1 + MLA_Q_LORA + MLA_KV_LORA]
    kr = w[:, s1 + MLA_Q_LORA + MLA_KV_LORA:s2]
    qk = 2 * DIFF_HEADS * DIFF_QK
    dq, dk, dv = w[:, s2:s2 + qk], w[:, s2 + qk:s2 + 2 * qk], w[:, s2 + 2 * qk:]
    zeros = lambda n: jnp.zeros((d, n), w.dtype)
    main = jnp.concatenate([z, xs, dq, dk, dv, cq, bm, cm, ckv, kr, zeros(LANES - MLA_ROPE), zeros(LANES)], axis=1)
    dtw = jnp.concatenate([dt, zeros(LANES - 2 * SSD_HEADS)], axis=1)
    return main.astype(BF16), dtw.astype(BF16)


def _layout_mla(w_uq, w_ukv, g_q, g_k):
    pad = MLA_QK_PAD - MLA_QK
    uq = jnp.pad(w_uq.reshape(MLA_Q_LORA, MLA_HEADS, MLA_QK), ((0, 0), (0, 0), (0, pad)))
    ukv = w_ukv.reshape(MLA_KV_LORA, MLA_HEADS, MLA_NOPE + MLA_V)
    return dict(
        w_uq=uq.reshape(MLA_Q_LORA, MLA_HEADS * MLA_QK_PAD).astype(BF16),
        w_ukn=ukv[:, :, :MLA_NOPE].reshape(MLA_KV_LORA, MLA_HEADS * MLA_NOPE).astype(BF16),
        w_uv=ukv[:, :, MLA_NOPE:].reshape(MLA_KV_LORA, MLA_HEADS * MLA_V).astype(BF16),
        g_q=jnp.pad(g_q, (0, pad)).reshape(1, MLA_QK_PAD),
        g_k=jnp.pad(g_k, (0, pad)).reshape(1, MLA_QK_PAD),
    )


def kernel(x, c, ctx, c_ctx, w_mod, b_mod, g_norm, w_ffn_gate, w_ffn_up, w_ffn_down, w_in, ssd_conv_w, ssd_conv_b, ssd_dt_bias, ssd_a_log, ssd_d, ssd_norm_g, mla_g_cq, mla_w_uq, mla_g_ckv, mla_w_ukv, mla_g_q, mla_g_k, diff_g_q, diff_g_k, diff_lambda, diff_subln_g, w_out):
    batch, seq, d = x.shape
    ctx_len = ctx.shape[1]
    depth = w_mod.shape[0]
    d_ff = w_ffn_gate.shape[-1]
    lat_rows, ctx_rows = batch * seq, batch * ctx_len
    n_rows = lat_rows + ctx_rows
    assert w_in.shape[-1] == IN_COLS and w_out.shape[1] == SSD_INNER + MLA_HEADS * MLA_V + DIFF_HEADS * DIFF_V
    assert seq % GRID_W == 0 and batch < MOD_ROWS
    tm = _pick(math.gcd(seq, ctx_rows), (512, 256, 128))
    tr = _pick(math.gcd(seq, ctx_len), (256, 128))
    assert seq % tm == 0 and ctx_rows % tm == 0 and seq % tr == 0 and ctx_len % tr == 0
    assert ctx_len % SSD_CHUNK == 0 and seq % SSD_CHUNK == 0 and lat_rows % ctx_len == 0
    tq = _pick(seq, (512, 256, 128))
    tk = _pick(seq // 2, (1024, 512, 256, 128))

    def group_of_row(r):
        return jnp.minimum(r // seq, batch)

    xs = jnp.concatenate([x.reshape(lat_rows, d), ctx.reshape(ctx_rows, d)], axis=0)
    c_rows = jnp.concatenate([c, c_ctx[None, :], jnp.zeros((MOD_ROWS - batch - 1, d), F32)], axis=0)
    mod = _modulation(c_rows, w_mod, b_mod).reshape(depth, MOD_ROWS, 1, N_MOD * d)

    rows = seq // GRID_W
    cos_m, sin_m = _token_tables(*_rope_tables(rows, MLA_ROPE), batch, ctx_rows, MLA_NOPE,
                                 MLA_QK_PAD - MLA_QK)
    cos_d, sin_d = _token_tables(*_rope_tables(rows, DIFF_QK), batch, ctx_rows, 0, 0)

    tn_ff = _pick(d_ff, (1024, 512, 256, 128))
    tn_d = _pick(d, (512, 256, 128))
    tn_in = _pick(PROJ_COLS, (768,))

    w_gate, w_up, w_down = w_ffn_gate.astype(BF16), w_ffn_up.astype(BF16), w_ffn_down.astype(BF16)

    def ffn(xs, li, j, norm_idx, gate_idx, m_rows):
        h = _norm_mod(xs, g_norm[li, norm_idx], mod[li], norm_idx, m_rows, group_of_row, tr)
        act = _matmul_swiglu(h, w_gate, w_up, (li, j), m_rows, tm, tn_ff)
        return _matmul_resid([(act, w_down, (li, j))], xs, mod[li], gate_idx, 0.5, m_rows, group_of_row, tm, tn_d)

    for li in range(depth):
        need_ctx = li < depth - 1
        out_rows = n_rows if need_ctx else lat_rows
        lam_init = 0.8 - 0.6 * math.exp(-0.3 * li)

        xs = ffn(xs, li, 0, 0, 2, n_rows)

        h = _norm_mod(xs, g_norm[li, 1], mod[li], 1, n_rows, group_of_row, tr)
        w_main, w_dt = _layout_w_in(w_in[li])
        proj = _matmul(h, w_main, n_rows, BF16, tm, tn_in)
        dt_raw = _matmul(h, w_dt, n_rows, F32, tm, LANES)

        xbc = _ssd_conv(proj, ssd_conv_w[li], ssd_conv_b[li], n_rows, lat_rows, seq, ctx_len, tr)
        yf, yb = _ssd_scan(xbc, dt_raw, ssd_dt_bias[li], ssd_a_log[li], batch, seq, ctx_len)
        ssd_o = _ssd_out(yf, yb, xbc, proj, ssd_d[li], ssd_norm_g[li], out_rows, tr)

        mla_p = _layout_mla(mla_w_uq[li], mla_w_ukv[li], mla_g_q[li], mla_g_k[li])
        mla_p['g_cq'] = mla_g_cq[li].reshape(1, MLA_Q_LORA)
        mla_p['g_ckv'] = mla_g_ckv[li].reshape(1, MLA_KV_LORA)
        mq, mk, mv = _mla_prep(proj, mla_p, cos_m, sin_m, n_rows, tr)
        mla_args = (_mla_finalize, 1, mq, mk, mv, [], MLA_HEADS, MLA_QK_PAD, MLA_V, 0, batch, seq, ctx_len)
        mla_o = _attention(*mla_args, True, tq, tk)

        dq, dk = _diff_prep(proj, diff_g_q[li], diff_g_k[li], cos_d, sin_d, n_rows, tr)
        diff_fin = functools.partial(_diff_finalize, lam_init=lam_init)
        extra = [diff_lambda[li], diff_subln_g[li].reshape(1, DIFF_V)]
        diff_args = (diff_fin, 2, dq, dk, proj, extra, DIFF_HEADS, 2 * DIFF_QK, DIFF_V, PC_DV // DIFF_V,
                     batch, seq, ctx_len)
        diff_o = _attention(*diff_args, True, tq, tk)
        if need_ctx:
            mla_o = jnp.concatenate([mla_o, _attention(*mla_args, False, tq, tk)], axis=0)
            diff_o = jnp.concatenate([diff_o, _attention(*diff_args, False, tq, tk)], axis=0)

        wo = w_out[li].astype(BF16)
        o1, o2 = SSD_INNER, SSD_INNER + MLA_HEADS * MLA_V
        xs = _matmul_resid([(ssd_o, wo[:o1], ()), (mla_o, wo[o1:o2], ()), (diff_o, wo[o2:], ())], xs, mod[li], 5,
                           1.0, out_rows, group_of_row, tm, tn_d)

        xs = ffn(xs, li, 1, 2, 8, out_rows)

    return xs[:lat_rows].reshape(batch, seq, d)
```

```python
import functools
import math

import jax
import jax.numpy as jnp
from jax import lax
from jax.experimental import pallas as pl
from jax.experimental.pallas import tpu as pltpu

F32 = jnp.float32
BF16 = jnp.bfloat16

GRID_W = 64
EPS = 1e-6
ROPE_THETA = 10000.0
N_MOD = 9
SSD_INNER = 2048
SSD_HEAD_DIM = 64
SSD_HEADS = 32
SSD_GROUPS = 4
SSD_STATE = 128
SSD_CONV = 5
SSD_CHUNK = 128
SSD_GN = SSD_GROUPS * SSD_STATE
SSD_CONV_CH = SSD_INNER + 2 * SSD_GN
SSD_COLS = 2 * SSD_INNER + 2 * SSD_GN + 2 * SSD_HEADS
MLA_HEADS = 8
MLA_NOPE = 128
MLA_ROPE = 64
MLA_V = 128
MLA_Q_LORA = 1024
MLA_KV_LORA = 512
MLA_QK = MLA_NOPE + MLA_ROPE
MLA_QK_PAD = 256
MLA_SCALE = 1.0 / math.sqrt(MLA_QK)
LOG2E = math.log2(math.e)
MLA_COLS = MLA_Q_LORA + MLA_KV_LORA + MLA_ROPE
DIFF_HEADS = 4
DIFF_QK = 128
DIFF_V = 2 * DIFF_QK
DIFF_SCALE = 1.0 / math.sqrt(DIFF_QK)
DIFF_COLS = DIFF_HEADS * (4 * DIFF_QK + DIFF_V)
IN_COLS = SSD_COLS + MLA_COLS + DIFF_COLS

LANES = 128
BF16_SUBLANES = 16
MOD_ROWS = 8
VMEM_LIMIT = 50 * 1024 * 1024

PC_Z = 0
PC_XS = 2048
PC_DQ = 4096
PC_DK = 5120
PC_DV = 6144
PC_CQ = 7168
PC_B = 8192
PC_C = 8704
PC_CKV = 9216
PC_KR = 9728
PC_PAD = 9856
PROJ_COLS = 9984


def _cparams(sem):
    return pltpu.CompilerParams(dimension_semantics=sem, vmem_limit_bytes=VMEM_LIMIT)


def _pick(n, prefs):
    for p in prefs:
        if n % p == 0:
            return p
    return n


def _silu(x):
    return x / (1.0 + jnp.exp(-x))


def _weight_spec(w, lead, tn):
    return pl.BlockSpec((None,) * len(lead) + (w.shape[-2], tn), lambda j, i: tuple(lead) + (0, j))


def _mm_kernel(a_ref, b_ref, o_ref):
    o_ref[...] = jnp.dot(a_ref[...], b_ref[...], preferred_element_type=F32).astype(o_ref.dtype)


def _matmul(a, b, m_rows, out_dtype, tm, tn):
    k = a.shape[1]
    n = b.shape[1]
    return pl.pallas_call(
        _mm_kernel,
        grid=(n // tn, m_rows // tm),
        in_specs=[pl.BlockSpec((tm, k), lambda j, i: (i, 0)),
                  pl.BlockSpec((k, tn), lambda j, i: (0, j))],
        out_specs=pl.BlockSpec((tm, tn), lambda j, i: (i, j)),
        out_shape=jax.ShapeDtypeStruct((m_rows, n), out_dtype),
        compiler_params=_cparams(("parallel", "parallel")),
        name="mm_plain",
    )(a, b)


def _swiglu_kernel(a_ref, wg_ref, wu_ref, o_ref):
    a = a_ref[...]
    g = jnp.dot(a, wg_ref[...], preferred_element_type=F32)
    u = jnp.dot(a, wu_ref[...], preferred_element_type=F32)
    o_ref[...] = (_silu(g) * u).astype(o_ref.dtype)


def _matmul_swiglu(a, wg, wu, lead, m_rows, tm, tn):
    k = a.shape[1]
    n = wg.shape[-1]
    return pl.pallas_call(
        _swiglu_kernel,
        grid=(n // tn, m_rows // tm),
        in_specs=[pl.BlockSpec((tm, k), lambda j, i: (i, 0)),
                  _weight_spec(wg, lead, tn), _weight_spec(wu, lead, tn)],
        out_specs=pl.BlockSpec((tm, tn), lambda j, i: (i, j)),
        out_shape=jax.ShapeDtypeStruct((m_rows, n), BF16),
        compiler_params=_cparams(("parallel", "parallel")),
        name="mm_swiglu",
    )(a, wg, wu)


def _resid_kernel(*refs, n_pairs, scale):
    x_ref, gate_ref, o_ref = refs[2 * n_pairs], refs[2 * n_pairs + 1], refs[2 * n_pairs + 2]
    acc = jnp.dot(refs[0][...], refs[1][...], preferred_element_type=F32)
    for p in range(1, n_pairs):
        acc = acc + jnp.dot(refs[2 * p][...], refs[2 * p + 1][...], preferred_element_type=F32)
    o_ref[...] = x_ref[...] + (scale * gate_ref[...]) * acc


def _matmul_resid(pairs, x, mod4, gate_idx, scale, m_rows, group_of_row, tm, tn):
    d = x.shape[1]
    in_specs = []
    args = []
    for a, b, lead in pairs:
        in_specs.append(pl.BlockSpec((tm, a.shape[1]), lambda j, i: (i, 0)))
        in_specs.append(_weight_spec(b, lead, tn))
        args += [a, b]
    in_specs.append(pl.BlockSpec((tm, tn), lambda j, i: (i, j)))
    gate_blk = gate_idx * (d // tn)
    in_specs.append(pl.BlockSpec((None, 1, tn), lambda j, i: (group_of_row(i * tm), 0, gate_blk + j)))
    args += [x, mod4]
    return pl.pallas_call(
        functools.partial(_resid_kernel, n_pairs=len(pairs), scale=scale),
        grid=(d // tn, m_rows // tm),
        in_specs=in_specs,
        out_specs=pl.BlockSpec((tm, tn), lambda j, i: (i, j)),
        out_shape=jax.ShapeDtypeStruct((m_rows, d), F32),
        compiler_params=_cparams(("parallel", "parallel")),
        name="mm_resid",
    )(*args)


def _mod_kernel(c_ref, w_ref, b_ref, o_ref):
    s = _silu(c_ref[...]).astype(BF16)
    w = w_ref[...].astype(BF16)
    o_ref[...] = jnp.dot(s, w, preferred_element_type=F32) + b_ref[...]


def _modulation(c_rows, w_mod, b_mod):
    depth, d, n = w_mod.shape
    tn = _pick(n, (512, 256, 128))
    return pl.pallas_call(
        _mod_kernel,
        grid=(depth, n // tn),
        in_specs=[pl.BlockSpec((MOD_ROWS, d), lambda l, j: (0, 0)),
                  pl.BlockSpec((None, d, tn), lambda l, j: (l, 0, j)),
                  pl.BlockSpec((None, 1, tn), lambda l, j: (l, 0, j))],
        out_specs=pl.BlockSpec((None, MOD_ROWS, tn), lambda l, j: (l, 0, j)),
        out_shape=jax.ShapeDtypeStruct((depth, MOD_ROWS, n), F32),
        compiler_params=_cparams(("parallel", "parallel")),
        name="modulation",
    )(c_rows, w_mod, b_mod.reshape(depth, 1, n))


def _norm_mod_kernel(x_ref, g_ref, shift_ref, scale_ref, o_ref):
    gain = g_ref[...] * (1.0 + scale_ref[...])
    shift = shift_ref[...]
    rows = BF16_SUBLANES

    def body(i, carry):
        r = pl.multiple_of(i * rows, rows)
        x = x_ref[pl.ds(r, rows), :]
        y = x * lax.rsqrt(jnp.mean(x * x, axis=-1, keepdims=True) + EPS)
        o_ref[pl.ds(r, rows), :] = (y * gain + shift).astype(o_ref.dtype)
        return carry

    lax.fori_loop(0, x_ref.shape[0] // rows, body, 0, unroll=2)


def _norm_mod(x, g, mod4, idx, m_rows, group_of_row, tm):
    d = x.shape[1]
    return pl.pallas_call(
        _norm_mod_kernel,
        grid=(m_rows // tm,),
        in_specs=[pl.BlockSpec((tm, d), lambda i: (i, 0)),
                  pl.BlockSpec((1, d), lambda i: (0, 0)),
                  pl.BlockSpec((None, 1, d), lambda i: (group_of_row(i * tm), 0, 3 * idx)),
                  pl.BlockSpec((None, 1, d), lambda i: (group_of_row(i * tm), 0, 3 * idx + 1))],
        out_specs=pl.BlockSpec((tm, d), lambda i: (i, 0)),
        out_shape=jax.ShapeDtypeStruct((m_rows, d), BF16),
        compiler_params=_cparams(("parallel",)),
        name="norm_mod",
    )(x, g.reshape(1, d), mod4, mod4)


def _conv_kernel(prev_ref, cur_ref, next_ref, w_ref, b_ref, o_ref, ext_ref, *, tr, lat_rows, seq, ctx_len):
    r0 = pl.program_id(0) * tr
    in_lat = r0 < lat_rows
    off = jnp.where(in_lat, r0 % seq, (r0 - lat_rows) % ctx_len)
    seg = jnp.where(in_lat, seq, ctx_len)
    first = off == 0
    last = off + tr == seg
    h = BF16_SUBLANES
    ext_ref[0:h, :] = jnp.where(first, 0.0, prev_ref[...].astype(F32))
    ext_ref[h:h + tr, :] = cur_ref[...].astype(F32)
    ext_ref[h + tr:2 * h + tr, :] = jnp.where(last, 0.0, next_ref[...].astype(F32))
    acc = b_ref[...] + w_ref[0:1, :] * ext_ref[pl.ds(h - 2, tr), :]
    for k in range(1, SSD_CONV):
        acc = acc + w_ref[k:k + 1, :] * ext_ref[pl.ds(h - 2 + k, tr), :]
    o_ref[...] = _silu(acc).astype(o_ref.dtype)


def _ssd_conv(proj, conv_w, conv_b, n_rows, lat_rows, seq, ctx_len, tr):
    tc = 1024
    h = BF16_SUBLANES
    rb = tr // h
    n_halo = n_rows // h
    xs_blk, bc_blk = PC_XS // tc, PC_B // tc
    n_xs = SSD_INNER // tc

    def col(j):
        return jnp.where(j < n_xs, xs_blk + j, bc_blk + (j - n_xs))

    return pl.pallas_call(
        functools.partial(_conv_kernel, tr=tr, lat_rows=lat_rows, seq=seq, ctx_len=ctx_len),
        grid=(n_rows // tr, SSD_CONV_CH // tc),
        in_specs=[pl.BlockSpec((h, tc), lambda i, j: (jnp.maximum(i * rb - 1, 0), col(j))),
                  pl.BlockSpec((tr, tc), lambda i, j: (i, col(j))),
                  pl.BlockSpec((h, tc), lambda i, j: (jnp.minimum((i + 1) * rb, n_halo - 1), col(j))),
                  pl.BlockSpec((SSD_CONV, tc), lambda i, j: (0, j)),
                  pl.BlockSpec((1, tc), lambda i, j: (0, j))],
        out_specs=pl.BlockSpec((tr, tc), lambda i, j: (i, j)),
        out_shape=jax.ShapeDtypeStruct((n_rows, SSD_CONV_CH), BF16),
        scratch_shapes=[pltpu.VMEM((tr + 2 * h, tc), F32)],
        compiler_params=_cparams(("parallel", "parallel")),
        name="ssd_conv",
    )(proj, proj, proj, conv_w, conv_b.reshape(1, SSD_CONV_CH))


N_PAIRS = SSD_HEADS // 2
PAIRS_PER_GROUP = N_PAIRS // SSD_GROUPS


def _ssd_direction(d, x_ref, b_ref, c_ref, dt_ref, bias_ref, alog_ref, y_ref, st_ref,
                   cs_ref, cst_ref, dtt_ref, woutt_ref, g_ref, bt_ref, yoff_ref, xp_ref, yp_ref):
    q = SSD_CHUNK
    lane = lax.broadcasted_iota(jnp.int32, (q, LANES), 1)
    row = lax.broadcasted_iota(jnp.int32, (q, q), 0)
    col = lax.broadcasted_iota(jnp.int32, (q, q), 1)
    mask = (row >= col) if d == 0 else (row <= col)

    z = dt_ref[...] + bias_ref[...]
    dt = jnp.maximum(z, 0.0) + jnp.log1p(jnp.exp(-jnp.abs(z)))
    a_dt = dt * (-jnp.exp(alog_ref[...]))
    hi = a_dt.astype(BF16)
    r1 = a_dt - hi.astype(F32)
    mid = r1.astype(BF16)
    lo = (r1 - mid.astype(F32)).astype(BF16)
    tri = jnp.where(mask, 1.0, 0.0).astype(BF16)
    c3 = jnp.dot(tri, jnp.concatenate([hi, mid, lo], axis=1), preferred_element_type=F32)
    cs = c3[:, 0:LANES] + c3[:, LANES:2 * LANES] + c3[:, 2 * LANES:3 * LANES]
    end = q - 1 if d == 0 else 0
    a_tot = cs[end:end + 1, :]
    cs_ref[...] = cs
    cst_ref[...] = cs.T
    dtt_ref[...] = dt.T
    woutt_ref[...] = (dt * jnp.exp(a_tot - cs)).T

    for g in range(SSD_GROUPS):
        bg = b_ref[:, g * SSD_STATE:(g + 1) * SSD_STATE]
        cg = c_ref[:, g * SSD_STATE:(g + 1) * SSD_STATE]
        g_ref[g] = lax.dot_general(cg, bg, (((1,), (1,)), ((), ())), preferred_element_type=F32)
        bt_ref[g] = bg.astype(F32).T
        for pp in range(PAIRS_PER_GROUP):
            p = g * PAIRS_PER_GROUP + pp
            yoff_ref[p] = jnp.dot(cg, st_ref[p].astype(BF16), preferred_element_type=F32)
    for p in range(N_PAIRS):
        xp_ref[p] = x_ref[:, p * LANES:(p + 1) * LANES]

    half = lane < SSD_HEAD_DIM

    def pair_body(p, carry):
        g = p // PAIRS_PER_GROUP
        h0 = 2 * p + SSD_HEADS * d
        gmat = g_ref[g]
        bt = bt_ref[g]
        xp = xp_ref[p]
        ys, ups, dins = [], [], []
        for hh in (h0, h0 + 1):
            cs_col = jnp.sum(jnp.where(lane == hh, cs_ref[...], 0.0), axis=1, keepdims=True)
            lmat = jnp.exp(jnp.where(mask, cs_col - cst_ref[pl.ds(hh, 1), :], -1e30))
            m = (gmat * lmat * dtt_ref[pl.ds(hh, 1), :]).astype(BF16)
            ys.append(jnp.dot(m, xp, preferred_element_type=F32))
            bw = (bt * woutt_ref[pl.ds(hh, 1), :]).astype(BF16)
            ups.append(jnp.dot(bw, xp, preferred_element_type=F32))
            dins.append(jnp.exp(cs_col))
        din = jnp.where(half, dins[0], dins[1])
        yp_ref[p] = jnp.where(half, ys[0], ys[1]) + yoff_ref[p] * din
        st_ref[p] = st_ref[p] * din[end:end + 1, :] + jnp.where(half, ups[0], ups[1])
        return carry

    lax.fori_loop(0, N_PAIRS, pair_body, 0, unroll=4)
    for p in range(N_PAIRS):
        y_ref[:, p * LANES:(p + 1) * LANES] = yp_ref[p]


def _ssd_kernel(xf_ref, bf_ref, cf_ref, dtf_ref, xb_ref, bb_ref, cb_ref, dtb_ref, bias_ref, alog_ref,
                yf_ref, yb_ref, stf_ref, stb_ref, cs_ref, cst_ref, dtt_ref, woutt_ref,
                g_ref, bt_ref, yoff_ref, xp_ref, yp_ref):
    @pl.when(pl.program_id(1) == 0)
    def _():
        stf_ref[...] = jnp.zeros_like(stf_ref)
        stb_ref[...] = jnp.zeros_like(stb_ref)

    shared = (cs_ref, cst_ref, dtt_ref, woutt_ref, g_ref, bt_ref, yoff_ref, xp_ref, yp_ref)
    _ssd_direction(0, xf_ref, bf_ref, cf_ref, dtf_ref, bias_ref, alog_ref, yf_ref, stf_ref, *shared)
    _ssd_direction(1, xb_ref, bb_ref, cb_ref, dtb_ref, bias_ref, alog_ref, yb_ref, stb_ref, *shared)


def _ssd_scan(xbc, dt_raw, dt_bias, a_log, batch, seq, ctx_len):
    q = SSD_CHUNK
    n_rows = xbc.shape[0]
    ncc, nlc = ctx_len // q, seq // q
    lat_blocks = batch * seq // q

    def fwd_blk(b, s):
        return jnp.where(s < ncc, lat_blocks + b * ncc + s, b * nlc + (s - ncc))

    def bwd_blk(b, s):
        return jnp.where(s < ncc, lat_blocks + b * ncc + (ncc - 1 - s), b * nlc + (nlc - 1 - (s - ncc)))

    def specs(blk):
        return [pl.BlockSpec((q, SSD_INNER), lambda b, s: (blk(b, s), 0)),
                pl.BlockSpec((q, SSD_GN), lambda b, s: (blk(b, s), SSD_INNER // SSD_GN)),
                pl.BlockSpec((q, SSD_GN), lambda b, s: (blk(b, s), SSD_INNER // SSD_GN + 1)),
                pl.BlockSpec((q, LANES), lambda b, s: (blk(b, s), 0))]

    pad = jnp.zeros((LANES - 2 * SSD_HEADS,), F32)
    bias = jnp.concatenate([dt_bias.reshape(-1), pad]).reshape(1, LANES)
    alog = jnp.concatenate([a_log.reshape(-1), pad]).reshape(1, LANES)
    tile = (q, LANES)
    return pl.pallas_call(
        _ssd_kernel,
        grid=(batch, ncc + nlc),
        in_specs=specs(fwd_blk) + specs(bwd_blk) + [pl.BlockSpec((1, LANES), lambda b, s: (0, 0))] * 2,
        out_specs=[pl.BlockSpec((q, SSD_INNER), lambda b, s: (fwd_blk(b, s), 0)),
                   pl.BlockSpec((q, SSD_INNER), lambda b, s: (bwd_blk(b, s), 0))],
        out_shape=[jax.ShapeDtypeStruct((n_rows, SSD_INNER), F32)] * 2,
        scratch_shapes=[pltpu.VMEM((N_PAIRS, SSD_STATE, LANES), F32),
                        pltpu.VMEM((N_PAIRS, SSD_STATE, LANES), F32),
                        pltpu.VMEM(tile, F32), pltpu.VMEM(tile, F32), pltpu.VMEM(tile, F32),
                        pltpu.VMEM(tile, F32),
                        pltpu.VMEM((SSD_GROUPS, q, q), F32),
                        pltpu.VMEM((SSD_GROUPS, SSD_STATE, q), F32),
                        pltpu.VMEM((N_PAIRS, q, LANES), F32),
                        pltpu.VMEM((N_PAIRS, q, LANES), BF16),
                        pltpu.VMEM((N_PAIRS, q, LANES), F32)],
        compiler_params=_cparams(("parallel", "arbitrary")),
        name="ssd_scan",
    )(xbc, xbc, xbc, dt_raw, xbc, xbc, xbc, dt_raw, bias, alog)


def _ssd_out_kernel(yf_ref, yb_ref, x_ref, z_ref, d_ref, g_ref, o_ref):
    z = z_ref[...].astype(F32)
    y = (yf_ref[...] + yb_ref[...] + d_ref[...] * x_ref[...].astype(F32)) * _silu(z)
    y = y * lax.rsqrt(jnp.mean(y * y, axis=-1, keepdims=True) + EPS) * g_ref[...]
    o_ref[...] = y.astype(o_ref.dtype)


def _ssd_out(yf, yb, xbc, proj, ssd_d, norm_g, n_rows, tm):
    w = SSD_INNER
    row = lambda i: (i, 0)
    return pl.pallas_call(
        _ssd_out_kernel,
        grid=(n_rows // tm,),
        in_specs=[pl.BlockSpec((tm, w), row), pl.BlockSpec((tm, w), row), pl.BlockSpec((tm, w), row),
                  pl.BlockSpec((tm, w), lambda i: (i, PC_Z // w)),
                  pl.BlockSpec((1, w), lambda i: (0, 0)), pl.BlockSpec((1, w), lambda i: (0, 0))],
        out_specs=pl.BlockSpec((tm, w), row),
        out_shape=jax.ShapeDtypeStruct((n_rows, w), BF16),
        compiler_params=_cparams(("parallel",)),
        name="ssd_out",
    )(yf, yb, xbc, proj, jnp.repeat(ssd_d, SSD_HEAD_DIM).reshape(1, w), norm_g.reshape(1, w))


def _rms(x, g, n):
    return x * lax.rsqrt(jnp.sum(x * x, axis=-1, keepdims=True) * (1.0 / n) + EPS) * g


def _rope(x, cos, sin_signed, half):
    w = x.shape[1]
    lane = lax.broadcasted_iota(jnp.int32, x.shape, 1)
    partner = jnp.where(lane % (2 * half) < half, pltpu.roll(x, w - half, 1), pltpu.roll(x, half, 1))
    return x * cos + partner * sin_signed


def _mla_prep_kernel(cq_ref, ckv_ref, kr_ref, gcq_ref, wuq_ref, gckv_ref, wukn_ref, wuv_ref, gq_ref, gk_ref,
                     cos_ref, sin_ref, q_ref, k_ref, v_ref):
    hq = _rms(cq_ref[...].astype(F32), gcq_ref[...], MLA_Q_LORA).astype(BF16)
    q_all = jnp.dot(hq, wuq_ref[...], preferred_element_type=F32)
    hkv = _rms(ckv_ref[...].astype(F32), gckv_ref[...], MLA_KV_LORA).astype(BF16)
    kn_all = jnp.dot(hkv, wukn_ref[...], preferred_element_type=F32)
    v_ref[...] = jnp.dot(hkv, wuv_ref[...], preferred_element_type=F32).astype(v_ref.dtype)
    kr = kr_ref[...].astype(F32)
    lane = lax.broadcasted_iota(jnp.int32, kr.shape, 1)
    kr = jnp.where(lane < MLA_ROPE, kr, 0.0)
    cos, sin = cos_ref[...], sin_ref[...]
    w = MLA_QK_PAD
    n = MLA_NOPE

    def norm_rope(lo, hi, g):
        ss = jnp.sum(lo * lo, axis=-1, keepdims=True) + jnp.sum(hi * hi, axis=-1, keepdims=True)
        r = lax.rsqrt(ss * (1.0 / MLA_QK) + EPS)
        return lo * r * g[:, :n], _rope(hi * r * g[:, n:], cos, sin, MLA_ROPE // 4)

    for h in range(MLA_HEADS):
        lo, hi = norm_rope(q_all[:, h * w:h * w + n], q_all[:, h * w + n:(h + 1) * w], gq_ref[...])
        q_ref[:, h * w:h * w + n] = (lo * (MLA_SCALE * LOG2E)).astype(q_ref.dtype)
        q_ref[:, h * w + n:(h + 1) * w] = (hi * (MLA_SCALE * LOG2E)).astype(q_ref.dtype)
        lo, hi = norm_rope(kn_all[:, h * n:(h + 1) * n], kr, gk_ref[...])
        k_ref[:, h * w:h * w + n] = lo.astype(k_ref.dtype)
        k_ref[:, h * w + n:(h + 1) * w] = hi.astype(k_ref.dtype)


def _mla_prep(proj, p, cos, sin, n_rows, tm):
    hw = MLA_HEADS * MLA_QK_PAD
    hv = MLA_HEADS * MLA_V
    full = lambda i: (0, 0)
    return pl.pallas_call(
        _mla_prep_kernel,
        grid=(n_rows // tm,),
        in_specs=[pl.BlockSpec((tm, MLA_Q_LORA), lambda i: (i, PC_CQ // MLA_Q_LORA)),
                  pl.BlockSpec((tm, MLA_KV_LORA), lambda i: (i, PC_CKV // MLA_KV_LORA)),
                  pl.BlockSpec((tm, LANES), lambda i: (i, PC_KR // LANES)),
                  pl.BlockSpec((1, MLA_Q_LORA), full), pl.BlockSpec((MLA_Q_LORA, hw), full),
                  pl.BlockSpec((1, MLA_KV_LORA), full), pl.BlockSpec((MLA_KV_LORA, hv), full),
                  pl.BlockSpec((MLA_KV_LORA, hv), full),
                  pl.BlockSpec((1, MLA_QK_PAD), full), pl.BlockSpec((1, MLA_QK_PAD), full),
                  pl.BlockSpec((tm, LANES), lambda i: (i, 0)),
                  pl.BlockSpec((tm, LANES), lambda i: (i, 0))],
        out_specs=[pl.BlockSpec((tm, hw), lambda i: (i, 0)), pl.BlockSpec((tm, hw), lambda i: (i, 0)),
                   pl.BlockSpec((tm, hv), lambda i: (i, 0))],
        out_shape=[jax.ShapeDtypeStruct((n_rows, hw), BF16), jax.ShapeDtypeStruct((n_rows, hw), BF16),
                   jax.ShapeDtypeStruct((n_rows, hv), BF16)],
        compiler_params=_cparams(("parallel",)),
        name="mla_prep",
    )(proj, proj, proj, p['g_cq'], p['w_uq'], p['g_ckv'], p['w_ukn'], p['w_uv'], p['g_q'], p['g_k'], cos, sin)


def _diff_prep_kernel(q_in, k_in, gq_ref, gk_ref, cos_ref, sin_ref, q_ref, k_ref):
    cos, sin = cos_ref[...], sin_ref[...]
    w = DIFF_QK
    for c in range(2 * DIFF_HEADS):
        sl = slice(c * w, (c + 1) * w)
        qc = _rope(_rms(q_in[:, sl].astype(F32), gq_ref[...], w), cos, sin, w // 4)
        q_ref[:, sl] = (qc * (DIFF_SCALE * LOG2E)).astype(q_ref.dtype)
        kc = _rope(_rms(k_in[:, sl].astype(F32), gk_ref[...], w), cos, sin, w // 4)
        k_ref[:, sl] = kc.astype(k_ref.dtype)


def _diff_prep(proj, g_q, g_k, cos, sin, n_rows, tm):
    w = 2 * DIFF_HEADS * DIFF_QK
    full = lambda i: (0, 0)
    row = lambda i: (i, 0)
    return pl.pallas_call(
        _diff_prep_kernel,
        grid=(n_rows // tm,),
        in_specs=[pl.BlockSpec((tm, w), lambda i: (i, PC_DQ // w)),
                  pl.BlockSpec((tm, w), lambda i: (i, PC_DK // w)),
                  pl.BlockSpec((1, DIFF_QK), full), pl.BlockSpec((1, DIFF_QK), full),
                  pl.BlockSpec((tm, DIFF_QK), row), pl.BlockSpec((tm, DIFF_QK), row)],
        out_specs=[pl.BlockSpec((tm, w), row), pl.BlockSpec((tm, w), row)],
        out_shape=[jax.ShapeDtypeStruct((n_rows, w), BF16)] * 2,
        compiler_params=_cparams(("parallel",)),
        name="diff_prep",
    )(proj, proj, g_q.reshape(1, DIFF_QK), g_k.reshape(1, DIFF_QK), cos, sin)


def _scores(q, k):
    return lax.dot_general(q, k, (((1,), (1,)), ((), ())), preferred_element_type=F32)


def _online_update(s, v, m_ref, l_ref, acc_ref, c):
    tiles = [s[:, t * LANES:(t + 1) * LANES] for t in range(s.shape[1] // LANES)]
    m_old = m_ref[c]
    m_new = jnp.maximum(m_old, jnp.max(functools.reduce(jnp.maximum, tiles), axis=1, keepdims=True))
    alpha = jnp.exp2(m_old - m_new)
    m_ref[c] = m_new
    ps = [jnp.exp2(t - m_new) for t in tiles]
    if l_ref is None:
        v = jnp.concatenate([v, jnp.ones((v.shape[0], LANES), v.dtype)], axis=1)
    else:
        l_ref[c] = alpha * l_ref[c] + functools.reduce(jnp.add, ps)
    pv = jnp.dot(jnp.concatenate(ps, axis=1).astype(BF16), v, preferred_element_type=F32)
    reps = acc_ref.shape[-1] // LANES
    acc_ref[c] = acc_ref[c] * jnp.concatenate([alpha] * reps, axis=1) + pv


def _attn_kernel(*refs, n_comp, n_lat, tk, rowsum_on_mxu, finalize):
    q_ref, kc_ref, vc_ref = refs[:3]
    n_in = 3
    if n_lat:
        kl_ref, vl_ref = refs[3:5]
        n_in = 5
    n_scratch = 3 if rowsum_on_mxu else 4
    extras = refs[n_in:-n_scratch - 1]
    o_ref, s_ref, m_ref, acc_ref = refs[-n_scratch - 1], refs[-n_scratch], refs[-n_scratch + 1], refs[-1]
    l_ref = None if rowsum_on_mxu else refs[-2]
    dk = q_ref.shape[1] // n_comp
    vw = o_ref.shape[1]
    m_ref[...] = jnp.full(m_ref.shape, -jnp.inf, F32)
    if l_ref is not None:
        l_ref[...] = jnp.zeros(l_ref.shape, F32)
    acc_ref[...] = jnp.zeros(acc_ref.shape, F32)

    def chunk_scores(k, c):
        return _scores(q_ref[:, c * dk:(c + 1) * dk], k[:, c * dk:(c + 1) * dk])

    def put_scores(slot, j):
        k = kl_ref[pl.ds(pl.multiple_of(j * tk, tk), tk), :]
        for c in range(n_comp):
            s_ref[slot, c] = chunk_scores(k, c)

    def consume(slot, j):
        v = vl_ref[pl.ds(pl.multiple_of(j * tk, tk), tk), :]
        for c in range(n_comp):
            _online_update(s_ref[slot, c], v, m_ref, l_ref, acc_ref, c)

    kc = kc_ref[...]
    s_ctx = [chunk_scores(kc, c) for c in range(n_comp)]
    if n_lat:
        put_scores(0, 0)
    for c in range(n_comp):
        _online_update(s_ctx[c], vc_ref[...], m_ref, l_ref, acc_ref, c)
    if n_lat:
        def pair(i, last):
            put_scores(1, 2 * i + 1)
            consume(0, 2 * i)
            if not last:
                put_scores(0, 2 * i + 2)
            consume(1, 2 * i + 1)

        def body(i, carry):
            pair(i, False)
            return carry

        lax.fori_loop(0, n_lat // 2 - 1, body, 0)
        pair(n_lat // 2 - 1, True)
    if rowsum_on_mxu:
        outs = [acc_ref[c][:, :vw] / acc_ref[c][:, vw:] for c in range(n_comp)]
    else:
        outs = [acc_ref[c] / jnp.sum(l_ref[c], axis=1, keepdims=True) for c in range(n_comp)]
    o_ref[...] = finalize(outs, *extras).astype(o_ref.dtype)


def _mla_finalize(outs):
    return outs[0]


def _diff_finalize(outs, lam_ref, g_ref, *, lam_init):
    lv = lam_ref[...]
    lam = (jnp.exp(jnp.sum(lv[0:1, :] * lv[1:2, :], axis=1, keepdims=True))
           - jnp.exp(jnp.sum(lv[2:3, :] * lv[3:4, :], axis=1, keepdims=True)) + lam_init)
    o = outs[0] - lam * outs[1]
    return o * lax.rsqrt(jnp.mean(o * o, axis=-1, keepdims=True) + EPS) * g_ref[...] * (1.0 - lam_init)


def _attention(finalize, n_comp, q, k, v, extra, heads, qw, vw, v_col0, batch, seq, ctx_len, latent, tq, tk):
    lat_ctx_blocks = batch * seq // ctx_len
    ctx_spec = lambda width, c0: pl.BlockSpec((ctx_len, width), lambda b, h, i: (lat_ctx_blocks + b, c0 + h))
    if latent:
        q_spec = pl.BlockSpec((tq, qw), lambda b, h, i: (b * (seq // tq) + i, h))
        in_specs = [q_spec, ctx_spec(qw, 0), ctx_spec(vw, v_col0),
                    pl.BlockSpec((seq, qw), lambda b, h, i: (b, h)),
                    pl.BlockSpec((seq, vw), lambda b, h, i: (b, v_col0 + h))]
        args = [q, k, v, k, v]
        grid = (batch, heads, seq // tq)
        out_spec = pl.BlockSpec((tq, vw), lambda b, h, i: (b * (seq // tq) + i, h))
        out_rows = batch * seq
        n_lat = seq // tk
        assert n_lat >= 2 and n_lat % 2 == 0
    else:
        tq = tk = ctx_len
        in_specs = [ctx_spec(qw, 0), ctx_spec(qw, 0), ctx_spec(vw, v_col0)]
        args = [q, k, v]
        grid = (batch, heads, 1)
        out_spec = pl.BlockSpec((ctx_len, vw), lambda b, h, i: (b, h))
        out_rows = batch * ctx_len
        n_lat = 0
    for e in extra:
        in_specs.append(pl.BlockSpec(e.shape, lambda b, h, i: (0, 0)))
        args.append(e)
    rowsum_on_mxu = vw == LANES
    scratch = [pltpu.VMEM((2, n_comp, tq, tk), F32),
               pltpu.VMEM((n_comp, tq, LANES), F32)]
    if rowsum_on_mxu:
        scratch.append(pltpu.VMEM((n_comp, tq, vw + LANES), F32))
    else:
        scratch += [pltpu.VMEM((n_comp, tq, LANES), F32),
                    pltpu.VMEM((n_comp, tq, vw), F32)]
    return pl.pallas_call(
        functools.partial(_attn_kernel, n_comp=n_comp, n_lat=n_lat, tk=tk, rowsum_on_mxu=rowsum_on_mxu,
                          finalize=finalize),
        grid=grid,
        in_specs=in_specs,
        out_specs=out_spec,
        out_shape=jax.ShapeDtypeStruct((out_rows, heads * vw), BF16),
        scratch_shapes=scratch,
        compiler_params=_cparams(("parallel", "parallel", "arbitrary")),
        name="attn_latent" if latent else "attn_ctx",
    )(*args)


def _rope_tables(rows, rot_dim):
    axis_dim = rot_dim // 2
    inv = ROPE_THETA ** (-jnp.arange(0, axis_dim, 2, dtype=F32) / axis_dim)
    row = jnp.repeat(jnp.arange(rows, dtype=F32), GRID_W)
    colv = (jnp.arange(rows * GRID_W) % GRID_W).astype(F32)
    ar, ac = row[:, None] * inv, colv[:, None] * inv
    cos = jnp.concatenate([jnp.cos(ar), jnp.cos(ar), jnp.cos(ac), jnp.cos(ac)], axis=1)
    sin = jnp.concatenate([-jnp.sin(ar), jnp.sin(ar), -jnp.sin(ac), jnp.sin(ac)], axis=1)
    return cos, sin


def _token_tables(cos, sin, batch, ctx_rows, left, right):
    n = cos.shape[0]
    cos = jnp.concatenate([jnp.ones((n, left), F32), cos, jnp.ones((n, right), F32)], axis=1)
    sin = jnp.concatenate([jnp.zeros((n, left), F32), sin, jnp.zeros((n, right), F32)], axis=1)
    w = cos.shape[1]
    cos = jnp.concatenate([jnp.tile(cos, (batch, 1)), jnp.ones((ctx_rows, w), F32)], axis=0)
    sin = jnp.concatenate([jnp.tile(sin, (batch, 1)), jnp.zeros((ctx_rows, w), F32)], axis=0)
    return cos, sin


def _layout_w_in(w):
    d = w.shape[0]
    s1, s2 = SSD_COLS, SSD_COLS + MLA_COLS
    xbc0 = SSD_INNER
    z = w[:, :SSD_INNER]
    xs = w[:, xbc0:xbc0 + SSD_INNER]
    bm = w[:, xbc0 + SSD_INNER:xbc0 + SSD_INNER + SSD_GN]
    cm = w[:, xbc0 + SSD_INNER + SSD_GN:xbc0 + SSD_CONV_CH]
    dt = w[:, xbc0 + SSD_CONV_CH:s1]
    cq = w[:, s1:s1 + MLA_Q_LORA]
    ckv = w[:, s1 + MLA_Q_LORA:s1 + MLA_Q_LORA + MLA_KV_LORA]
    kr = w[:, s1 + MLA_Q_LORA + MLA_KV_LORA:s2]
    qk = 2 * DIFF_HEADS * DIFF_QK
    dq, dk, dv = w[:, s2:s2 + qk], w[:, s2 + qk:s2 + 2 * qk], w[:, s2 + 2 * qk:]
    zeros = lambda n: jnp.zeros((d, n), w.dtype)
    main = jnp.concatenate([z, xs, dq, dk, dv, cq, bm, cm, ckv, kr, zeros(LANES - MLA_ROPE), zeros(LANES)], axis=1)
    dtw = jnp.concatenate([dt, zeros(LANES - 2 * SSD_HEADS)], axis=1)
    return main.astype(BF16), dtw.astype(BF16)


def _layout_mla(w_uq, w_ukv, g_q, g_k):
    pad = MLA_QK_PAD - MLA_QK
    uq = jnp.pad(w_uq.reshape(MLA_Q_LORA, MLA_HEADS, MLA_QK), ((0, 0), (0, 0), (0, pad)))
    ukv = w_ukv.reshape(MLA_KV_LORA, MLA_HEADS, MLA_NOPE + MLA_V)
    return dict(
        w_uq=uq.reshape(MLA_Q_LORA, MLA_HEADS * MLA_QK_PAD).astype(BF16),
        w_ukn=ukv[:, :, :MLA_NOPE].reshape(MLA_KV_LORA, MLA_HEADS * MLA_NOPE).astype(BF16),
        w_uv=ukv[:, :, MLA_NOPE:].reshape(MLA_KV_LORA, MLA_HEADS * MLA_V).astype(BF16),
        g_q=jnp.pad(g_q, (0, pad)).reshape(1, MLA_QK_PAD),
        g_k=jnp.pad(g_k, (0, pad)).reshape(1, MLA_QK_PAD),
    )


def kernel(x, c, ctx, c_ctx, w_mod, b_mod, g_norm, w_ffn_gate, w_ffn_up, w_ffn_down, w_in, ssd_conv_w, ssd_conv_b, ssd_dt_bias, ssd_a_log, ssd_d, ssd_norm_g, mla_g_cq, mla_w_uq, mla_g_ckv, mla_w_ukv, mla_g_q, mla_g_k, diff_g_q, diff_g_k, diff_lambda, diff_subln_g, w_out):
    batch, seq, d = x.shape
    ctx_len = ctx.shape[1]
    depth = w_mod.shape[0]
    d_ff = w_ffn_gate.shape[-1]
    lat_rows, ctx_rows = batch * seq, batch * ctx_len
    n_rows = lat_rows + ctx_rows
    assert w_in.shape[-1] == IN_COLS and w_out.shape[1] == SSD_INNER + MLA_HEADS * MLA_V + DIFF_HEADS * DIFF_V
    assert seq % GRID_W == 0 and batch < MOD_ROWS
    tm = _pick(math.gcd(seq, ctx_rows), (512, 256, 128))
    tr = _pick(math.gcd(seq, ctx_len), (256, 128))
    assert seq % tm == 0 and ctx_rows % tm == 0 and seq % tr == 0 and ctx_len % tr == 0
    assert ctx_len % SSD_CHUNK == 0 and seq % SSD_CHUNK == 0 and lat_rows % ctx_len == 0
    tq = _pick(seq, (512, 256, 128))
    tk = _pick(seq // 2, (1024, 512, 256, 128))

    def group_of_row(r):
        return jnp.minimum(r // seq, batch)

    xs = jnp.concatenate([x.reshape(lat_rows, d), ctx.reshape(ctx_rows, d)], axis=0)
    c_rows = jnp.concatenate([c, c_ctx[None, :], jnp.zeros((MOD_ROWS - batch - 1, d), F32)], axis=0)
    mod = _modulation(c_rows, w_mod, b_mod).reshape(depth, MOD_ROWS, 1, N_MOD * d)

    rows = seq // GRID_W
    cos_m, sin_m = _token_tables(*_rope_tables(rows, MLA_ROPE), batch, ctx_rows, 0, LANES - MLA_ROPE)
    cos_d, sin_d = _token_tables(*_rope_tables(rows, DIFF_QK), batch, ctx_rows, 0, 0)

    tn_ff = _pick(d_ff, (1024, 512, 256, 128))
    tn_d = _pick(d, (512, 256, 128))
    tn_out = _pick(d, (1024, 512, 256, 128))
    tn_in = _pick(PROJ_COLS, (768,))

    w_gate, w_up, w_down = w_ffn_gate.astype(BF16), w_ffn_up.astype(BF16), w_ffn_down.astype(BF16)

    def ffn(xs, li, j, norm_idx, gate_idx, m_rows):
        h = _norm_mod(xs, g_norm[li, norm_idx], mod[li], norm_idx, m_rows, group_of_row, tr)
        act = _matmul_swiglu(h, w_gate, w_up, (li, j), m_rows, tm, tn_ff)
        return _matmul_resid([(act, w_down, (li, j))], xs, mod[li], gate_idx, 0.5, m_rows, group_of_row, tm, tn_d)

    for li in range(depth):
        need_ctx = li < depth - 1
        out_rows = n_rows if need_ctx else lat_rows
        lam_init = 0.8 - 0.6 * math.exp(-0.3 * li)

        xs = ffn(xs, li, 0, 0, 2, n_rows)

        h = _norm_mod(xs, g_norm[li, 1], mod[li], 1, n_rows, group_of_row, tr)
        w_main, w_dt = _layout_w_in(w_in[li])
        proj = _matmul(h, w_main, n_rows, BF16, tm, tn_in)
        dt_raw = _matmul(h, w_dt, n_rows, F32, tm, LANES)

        xbc = _ssd_conv(proj, ssd_conv_w[li], ssd_conv_b[li], n_rows, lat_rows, seq, ctx_len, tr)
        yf, yb = _ssd_scan(xbc, dt_raw, ssd_dt_bias[li], ssd_a_log[li], batch, seq, ctx_len)
        ssd_o = _ssd_out(yf, yb, xbc, proj, ssd_d[li], ssd_norm_g[li], out_rows, tr)

        mla_p = _layout_mla(mla_w_uq[li], mla_w_ukv[li], mla_g_q[li], mla_g_k[li])
        mla_p['g_cq'] = mla_g_cq[li].reshape(1, MLA_Q_LORA)
        mla_p['g_ckv'] = mla_g_ckv[li].reshape(1, MLA_KV_LORA)
        mq, mk, mv = _mla_prep(proj, mla_p, cos_m, sin_m, n_rows, tr)
        mla_args = (_mla_finalize, 1, mq, mk, mv, [], MLA_HEADS, MLA_QK_PAD, MLA_V, 0, batch, seq, ctx_len)
        mla_o = _attention(*mla_args, True, tq, tk)

        dq, dk = _diff_prep(proj, diff_g_q[li], diff_g_k[li], cos_d, sin_d, n_rows, tr)
        diff_fin = functools.partial(_diff_finalize, lam_init=lam_init)
        extra = [diff_lambda[li], diff_subln_g[li].reshape(1, DIFF_V)]
        diff_args = (diff_fin, 2, dq, dk, proj, extra, DIFF_HEADS, 2 * DIFF_QK, DIFF_V, PC_DV // DIFF_V,
                     batch, seq, ctx_len)
        diff_o = _attention(*diff_args, True, tq, tk)
        if need_ctx:
            mla_o = jnp.concatenate([mla_o, _attention(*mla_args, False, tq, tk)], axis=0)
            diff_o = jnp.concatenate([diff_o, _attention(*diff_args, False, tq, tk)], axis=0)

        wo = w_out[li].astype(BF16)
        o1, o2 = SSD_INNER, SSD_INNER + MLA_HEADS * MLA_V
        xs = _matmul_resid([(ssd_o, wo[:o1], ()), (mla_o, wo[o1:o2], ()), (diff_o, wo[o2:], ())], xs, mod[li], 5,
                           1.0, out_rows, group_of_row, tm, tn_out)

        xs = ffn(xs, li, 1, 2, 8, out_rows)

    return xs[:lat_rows].reshape(batch, seq, d)
```

```python
import functools
import math

import jax
import jax.numpy as jnp
from jax import lax
from jax.experimental import pallas as pl
from jax.experimental.pallas import tpu as pltpu

F32 = jnp.float32
BF16 = jnp.bfloat16

GRID_W = 64
EPS = 1e-6
ROPE_THETA = 10000.0
N_MOD = 9
SSD_INNER = 2048
SSD_HEAD_DIM = 64
SSD_HEADS = 32
SSD_GROUPS = 4
SSD_STATE = 128
SSD_CONV = 5
SSD_CHUNK = 128
SSD_GN = SSD_GROUPS * SSD_STATE
SSD_CONV_CH = SSD_INNER + 2 * SSD_GN
SSD_COLS = 2 * SSD_INNER + 2 * SSD_GN + 2 * SSD_HEADS
MLA_HEADS = 8
MLA_NOPE = 128
MLA_ROPE = 64
MLA_V = 128
MLA_Q_LORA = 1024
MLA_KV_LORA = 512
MLA_QK = MLA_NOPE + MLA_ROPE
MLA_QK_PAD = 256
MLA_SCALE = 1.0 / math.sqrt(MLA_QK)
LOG2E = math.log2(math.e)
MLA_COLS = MLA_Q_LORA + MLA_KV_LORA + MLA_ROPE
DIFF_HEADS = 4
DIFF_QK = 128
DIFF_V = 2 * DIFF_QK
DIFF_SCALE = 1.0 / math.sqrt(DIFF_QK)
DIFF_COLS = DIFF_HEADS * (4 * DIFF_QK + DIFF_V)
IN_COLS = SSD_COLS + MLA_COLS + DIFF_COLS

LANES = 128
BF16_SUBLANES = 16
MOD_ROWS = 8
VMEM_LIMIT = 50 * 1024 * 1024

PC_Z = 0
PC_XS = 2048
PC_DQ = 4096
PC_DK = 5120
PC_DV = 6144
PC_CQ = 7168
PC_B = 8192
PC_C = 8704
PC_CKV = 9216
PC_KR = 9728
PC_PAD = 9856
PROJ_COLS = 9984


def _cparams(sem):
    return pltpu.CompilerParams(dimension_semantics=sem, vmem_limit_bytes=VMEM_LIMIT)


def _pick(n, prefs):
    for p in prefs:
        if n % p == 0:
            return p
    return n


def _silu(x):
    return x / (1.0 + jnp.exp(-x))


def _weight_spec(w, lead, tn):
    return pl.BlockSpec((None,) * len(lead) + (w.shape[-2], tn), lambda j, i: tuple(lead) + (0, j))


def _mm_kernel(a_ref, b_ref, o_ref):
    o_ref[...] = jnp.dot(a_ref[...], b_ref[...], preferred_element_type=F32).astype(o_ref.dtype)


def _matmul(a, b, m_rows, out_dtype, tm, tn):
    k = a.shape[1]
    n = b.shape[1]
    return pl.pallas_call(
        _mm_kernel,
        grid=(n // tn, m_rows // tm),
        in_specs=[pl.BlockSpec((tm, k), lambda j, i: (i, 0)),
                  pl.BlockSpec((k, tn), lambda j, i: (0, j))],
        out_specs=pl.BlockSpec((tm, tn), lambda j, i: (i, j)),
        out_shape=jax.ShapeDtypeStruct((m_rows, n), out_dtype),
        compiler_params=_cparams(("parallel", "parallel")),
        name="mm_plain",
    )(a, b)


def _swiglu_kernel(a_ref, wg_ref, wu_ref, o_ref):
    a = a_ref[...]
    g = jnp.dot(a, wg_ref[...], preferred_element_type=F32)
    u = jnp.dot(a, wu_ref[...], preferred_element_type=F32)
    o_ref[...] = (_silu(g) * u).astype(o_ref.dtype)


def _matmul_swiglu(a, wg, wu, lead, m_rows, tm, tn):
    k = a.shape[1]
    n = wg.shape[-1]
    return pl.pallas_call(
        _swiglu_kernel,
        grid=(n // tn, m_rows // tm),
        in_specs=[pl.BlockSpec((tm, k), lambda j, i: (i, 0)),
                  _weight_spec(wg, lead, tn), _weight_spec(wu, lead, tn)],
        out_specs=pl.BlockSpec((tm, tn), lambda j, i: (i, j)),
        out_shape=jax.ShapeDtypeStruct((m_rows, n), BF16),
        compiler_params=_cparams(("parallel", "parallel")),
        name="mm_swiglu",
    )(a, wg, wu)


def _resid_kernel(*refs, n_pairs, scale):
    x_ref, gate_ref, o_ref = refs[2 * n_pairs], refs[2 * n_pairs + 1], refs[2 * n_pairs + 2]
    acc = jnp.dot(refs[0][...], refs[1][...], preferred_element_type=F32)
    for p in range(1, n_pairs):
        acc = acc + jnp.dot(refs[2 * p][...], refs[2 * p + 1][...], preferred_element_type=F32)
    o_ref[...] = x_ref[...] + (scale * gate_ref[...]) * acc


def _matmul_resid(pairs, x, mod4, gate_idx, scale, m_rows, group_of_row, tm, tn):
    d = x.shape[1]
    in_specs = []
    args = []
    for a, b, lead in pairs:
        in_specs.append(pl.BlockSpec((tm, a.shape[1]), lambda j, i: (i, 0)))
        in_specs.append(_weight_spec(b, lead, tn))
        args += [a, b]
    in_specs.append(pl.BlockSpec((tm, tn), lambda j, i: (i, j)))
    gate_blk = gate_idx * (d // tn)
    in_specs.append(pl.BlockSpec((None, 1, tn), lambda j, i: (group_of_row(i * tm), 0, gate_blk + j)))
    args += [x, mod4]
    return pl.pallas_call(
        functools.partial(_resid_kernel, n_pairs=len(pairs), scale=scale),
        grid=(d // tn, m_rows // tm),
        in_specs=in_specs,
        out_specs=pl.BlockSpec((tm, tn), lambda j, i: (i, j)),
        out_shape=jax.ShapeDtypeStruct((m_rows, d), F32),
        compiler_params=_cparams(("parallel", "parallel")),
        name="mm_resid",
    )(*args)


def _mod_kernel(c_ref, w_ref, b_ref, o_ref):
    s = _silu(c_ref[...]).astype(BF16)
    w = w_ref[...].astype(BF16)
    o_ref[...] = jnp.dot(s, w, preferred_element_type=F32) + b_ref[...]


def _modulation(c_rows, w_mod, b_mod):
    depth, d, n = w_mod.shape
    tn = _pick(n, (512, 256, 128))
    return pl.pallas_call(
        _mod_kernel,
        grid=(depth, n // tn),
        in_specs=[pl.BlockSpec((MOD_ROWS, d), lambda l, j: (0, 0)),
                  pl.BlockSpec((None, d, tn), lambda l, j: (l, 0, j)),
                  pl.BlockSpec((None, 1, tn), lambda l, j: (l, 0, j))],
        out_specs=pl.BlockSpec((None, MOD_ROWS, tn), lambda l, j: (l, 0, j)),
        out_shape=jax.ShapeDtypeStruct((depth, MOD_ROWS, n), F32),
        compiler_params=_cparams(("parallel", "parallel")),
        name="modulation",
    )(c_rows, w_mod, b_mod.reshape(depth, 1, n))


def _norm_mod_kernel(x_ref, g_ref, shift_ref, scale_ref, o_ref):
    gain = g_ref[...] * (1.0 + scale_ref[...])
    shift = shift_ref[...]
    rows = BF16_SUBLANES

    def body(i, carry):
        r = pl.multiple_of(i * rows, rows)
        x = x_ref[pl.ds(r, rows), :]
        y = x * lax.rsqrt(jnp.mean(x * x, axis=-1, keepdims=True) + EPS)
        o_ref[pl.ds(r, rows), :] = (y * gain + shift).astype(o_ref.dtype)
        return carry

    lax.fori_loop(0, x_ref.shape[0] // rows, body, 0, unroll=2)


def _norm_mod(x, g, mod4, idx, m_rows, group_of_row, tm):
    d = x.shape[1]
    return pl.pallas_call(
        _norm_mod_kernel,
        grid=(m_rows // tm,),
        in_specs=[pl.BlockSpec((tm, d), lambda i: (i, 0)),
                  pl.BlockSpec((1, d), lambda i: (0, 0)),
                  pl.BlockSpec((None, 1, d), lambda i: (group_of_row(i * tm), 0, 3 * idx)),
                  pl.BlockSpec((None, 1, d), lambda i: (group_of_row(i * tm), 0, 3 * idx + 1))],
        out_specs=pl.BlockSpec((tm, d), lambda i: (i, 0)),
        out_shape=jax.ShapeDtypeStruct((m_rows, d), BF16),
        compiler_params=_cparams(("parallel",)),
        name="norm_mod",
    )(x, g.reshape(1, d), mod4, mod4)


def _conv_kernel(prev_ref, cur_ref, next_ref, w_ref, b_ref, o_ref, ext_ref, *, tr, lat_rows, seq, ctx_len):
    r0 = pl.program_id(0) * tr
    in_lat = r0 < lat_rows
    off = jnp.where(in_lat, r0 % seq, (r0 - lat_rows) % ctx_len)
    seg = jnp.where(in_lat, seq, ctx_len)
    first = off == 0
    last = off + tr == seg
    h = BF16_SUBLANES
    ext_ref[0:h, :] = jnp.where(first, 0.0, prev_ref[...].astype(F32))
    ext_ref[h:h + tr, :] = cur_ref[...].astype(F32)
    ext_ref[h + tr:2 * h + tr, :] = jnp.where(last, 0.0, next_ref[...].astype(F32))
    acc = b_ref[...] + w_ref[0:1, :] * ext_ref[pl.ds(h - 2, tr), :]
    for k in range(1, SSD_CONV):
        acc = acc + w_ref[k:k + 1, :] * ext_ref[pl.ds(h - 2 + k, tr), :]
    o_ref[...] = _silu(acc).astype(o_ref.dtype)


def _ssd_conv(proj, conv_w, conv_b, n_rows, lat_rows, seq, ctx_len, tr):
    tc = 1024
    h = BF16_SUBLANES
    rb = tr // h
    n_halo = n_rows // h
    xs_blk, bc_blk = PC_XS // tc, PC_B // tc
    n_xs = SSD_INNER // tc

    def col(j):
        return jnp.where(j < n_xs, xs_blk + j, bc_blk + (j - n_xs))

    return pl.pallas_call(
        functools.partial(_conv_kernel, tr=tr, lat_rows=lat_rows, seq=seq, ctx_len=ctx_len),
        grid=(n_rows // tr, SSD_CONV_CH // tc),
        in_specs=[pl.BlockSpec((h, tc), lambda i, j: (jnp.maximum(i * rb - 1, 0), col(j))),
                  pl.BlockSpec((tr, tc), lambda i, j: (i, col(j))),
                  pl.BlockSpec((h, tc), lambda i, j: (jnp.minimum((i + 1) * rb, n_halo - 1), col(j))),
                  pl.BlockSpec((SSD_CONV, tc), lambda i, j: (0, j)),
                  pl.BlockSpec((1, tc), lambda i, j: (0, j))],
        out_specs=pl.BlockSpec((tr, tc), lambda i, j: (i, j)),
        out_shape=jax.ShapeDtypeStruct((n_rows, SSD_CONV_CH), BF16),
        scratch_shapes=[pltpu.VMEM((tr + 2 * h, tc), F32)],
        compiler_params=_cparams(("parallel", "parallel")),
        name="ssd_conv",
    )(proj, proj, proj, conv_w, conv_b.reshape(1, SSD_CONV_CH))


N_PAIRS = SSD_HEADS // 2
PAIRS_PER_GROUP = N_PAIRS // SSD_GROUPS


def _ssd_direction(d, x_ref, b_ref, c_ref, dt_ref, bias_ref, alog_ref, y_ref, st_ref,
                   cs_ref, cst_ref, dtt_ref, woutt_ref, g_ref, bt_ref, yoff_ref, xp_ref, yp_ref):
    q = SSD_CHUNK
    lane = lax.broadcasted_iota(jnp.int32, (q, LANES), 1)
    row = lax.broadcasted_iota(jnp.int32, (q, q), 0)
    col = lax.broadcasted_iota(jnp.int32, (q, q), 1)
    mask = (row >= col) if d == 0 else (row <= col)

    z = dt_ref[...] + bias_ref[...]
    dt = jnp.maximum(z, 0.0) + jnp.log1p(jnp.exp(-jnp.abs(z)))
    a_dt = dt * (-jnp.exp(alog_ref[...]))
    hi = a_dt.astype(BF16)
    r1 = a_dt - hi.astype(F32)
    mid = r1.astype(BF16)
    lo = (r1 - mid.astype(F32)).astype(BF16)
    tri = jnp.where(mask, 1.0, 0.0).astype(BF16)
    c3 = jnp.dot(tri, jnp.concatenate([hi, mid, lo], axis=1), preferred_element_type=F32)
    cs = c3[:, 0:LANES] + c3[:, LANES:2 * LANES] + c3[:, 2 * LANES:3 * LANES]
    end = q - 1 if d == 0 else 0
    a_tot = cs[end:end + 1, :]
    cs_ref[...] = cs
    cst_ref[...] = cs.T
    dtt_ref[...] = dt.T
    woutt_ref[...] = (dt * jnp.exp(a_tot - cs)).T

    for g in range(SSD_GROUPS):
        bg = b_ref[:, g * SSD_STATE:(g + 1) * SSD_STATE]
        cg = c_ref[:, g * SSD_STATE:(g + 1) * SSD_STATE]
        g_ref[g] = lax.dot_general(cg, bg, (((1,), (1,)), ((), ())), preferred_element_type=F32)
        bt_ref[g] = bg.astype(F32).T
        for pp in range(PAIRS_PER_GROUP):
            p = g * PAIRS_PER_GROUP + pp
            yoff_ref[p] = jnp.dot(cg, st_ref[p].astype(BF16), preferred_element_type=F32)
    for p in range(N_PAIRS):
        xp_ref[p] = x_ref[:, p * LANES:(p + 1) * LANES]

    half = lane < SSD_HEAD_DIM

    def pair_body(p, carry):
        g = p // PAIRS_PER_GROUP
        h0 = 2 * p + SSD_HEADS * d
        gmat = g_ref[g]
        bt = bt_ref[g]
        xp = xp_ref[p]
        ys, ups, dins = [], [], []
        for hh in (h0, h0 + 1):
            cs_col = jnp.sum(jnp.where(lane == hh, cs_ref[...], 0.0), axis=1, keepdims=True)
            lmat = jnp.exp(jnp.where(mask, cs_col - cst_ref[pl.ds(hh, 1), :], -1e30))
            m = (gmat * lmat * dtt_ref[pl.ds(hh, 1), :]).astype(BF16)
            ys.append(jnp.dot(m, xp, preferred_element_type=F32))
            bw = (bt * woutt_ref[pl.ds(hh, 1), :]).astype(BF16)
            ups.append(jnp.dot(bw, xp, preferred_element_type=F32))
            dins.append(jnp.exp(cs_col))
        din = jnp.where(half, dins[0], dins[1])
        yp_ref[p] = jnp.where(half, ys[0], ys[1]) + yoff_ref[p] * din
        st_ref[p] = st_ref[p] * din[end:end + 1, :] + jnp.where(half, ups[0], ups[1])
        return carry

    lax.fori_loop(0, N_PAIRS, pair_body, 0, unroll=4)
    for p in range(N_PAIRS):
        y_ref[:, p * LANES:(p + 1) * LANES] = yp_ref[p]


def _ssd_kernel(xf_ref, bf_ref, cf_ref, dtf_ref, xb_ref, bb_ref, cb_ref, dtb_ref, bias_ref, alog_ref,
                yf_ref, yb_ref, stf_ref, stb_ref, cs_ref, cst_ref, dtt_ref, woutt_ref,
                g_ref, bt_ref, yoff_ref, xp_ref, yp_ref):
    @pl.when(pl.program_id(1) == 0)
    def _():
        stf_ref[...] = jnp.zeros_like(stf_ref)
        stb_ref[...] = jnp.zeros_like(stb_ref)

    shared = (cs_ref, cst_ref, dtt_ref, woutt_ref, g_ref, bt_ref, yoff_ref, xp_ref, yp_ref)
    _ssd_direction(0, xf_ref, bf_ref, cf_ref, dtf_ref, bias_ref, alog_ref, yf_ref, stf_ref, *shared)
    _ssd_direction(1, xb_ref, bb_ref, cb_ref, dtb_ref, bias_ref, alog_ref, yb_ref, stb_ref, *shared)


def _ssd_scan(xbc, dt_raw, dt_bias, a_log, batch, seq, ctx_len):
    q = SSD_CHUNK
    n_rows = xbc.shape[0]
    ncc, nlc = ctx_len // q, seq // q
    lat_blocks = batch * seq // q

    def fwd_blk(b, s):
        return jnp.where(s < ncc, lat_blocks + b * ncc + s, b * nlc + (s - ncc))

    def bwd_blk(b, s):
        return jnp.where(s < ncc, lat_blocks + b * ncc + (ncc - 1 - s), b * nlc + (nlc - 1 - (s - ncc)))

    def specs(blk):
        return [pl.BlockSpec((q, SSD_INNER), lambda b, s: (blk(b, s), 0)),
                pl.BlockSpec((q, SSD_GN), lambda b, s: (blk(b, s), SSD_INNER // SSD_GN)),
                pl.BlockSpec((q, SSD_GN), lambda b, s: (blk(b, s), SSD_INNER // SSD_GN + 1)),
                pl.BlockSpec((q, LANES), lambda b, s: (blk(b, s), 0))]

    pad = jnp.zeros((LANES - 2 * SSD_HEADS,), F32)
    bias = jnp.concatenate([dt_bias.reshape(-1), pad]).reshape(1, LANES)
    alog = jnp.concatenate([a_log.reshape(-1), pad]).reshape(1, LANES)
    tile = (q, LANES)
    return pl.pallas_call(
        _ssd_kernel,
        grid=(batch, ncc + nlc),
        in_specs=specs(fwd_blk) + specs(bwd_blk) + [pl.BlockSpec((1, LANES), lambda b, s: (0, 0))] * 2,
        out_specs=[pl.BlockSpec((q, SSD_INNER), lambda b, s: (fwd_blk(b, s), 0)),
                   pl.BlockSpec((q, SSD_INNER), lambda b, s: (bwd_blk(b, s), 0))],
        out_shape=[jax.ShapeDtypeStruct((n_rows, SSD_INNER), F32)] * 2,
        scratch_shapes=[pltpu.VMEM((N_PAIRS, SSD_STATE, LANES), F32),
                        pltpu.VMEM((N_PAIRS, SSD_STATE, LANES), F32),
                        pltpu.VMEM(tile, F32), pltpu.VMEM(tile, F32), pltpu.VMEM(tile, F32),
                        pltpu.VMEM(tile, F32),
                        pltpu.VMEM((SSD_GROUPS, q, q), F32),
                        pltpu.VMEM((SSD_GROUPS, SSD_STATE, q), F32),
                        pltpu.VMEM((N_PAIRS, q, LANES), F32),
                        pltpu.VMEM((N_PAIRS, q, LANES), BF16),
                        pltpu.VMEM((N_PAIRS, q, LANES), F32)],
        compiler_params=_cparams(("parallel", "arbitrary")),
        name="ssd_scan",
    )(xbc, xbc, xbc, dt_raw, xbc, xbc, xbc, dt_raw, bias, alog)


def _ssd_out_kernel(yf_ref, yb_ref, x_ref, z_ref, d_ref, g_ref, o_ref):
    z = z_ref[...].astype(F32)
    y = (yf_ref[...] + yb_ref[...] + d_ref[...] * x_ref[...].astype(F32)) * _silu(z)
    y = y * lax.rsqrt(jnp.mean(y * y, axis=-1, keepdims=True) + EPS) * g_ref[...]
    o_ref[...] = y.astype(o_ref.dtype)


def _ssd_out(yf, yb, xbc, proj, ssd_d, norm_g, n_rows, tm):
    w = SSD_INNER
    row = lambda i: (i, 0)
    return pl.pallas_call(
        _ssd_out_kernel,
        grid=(n_rows // tm,),
        in_specs=[pl.BlockSpec((tm, w), row), pl.BlockSpec((tm, w), row), pl.BlockSpec((tm, w), row),
                  pl.BlockSpec((tm, w), lambda i: (i, PC_Z // w)),
                  pl.BlockSpec((1, w), lambda i: (0, 0)), pl.BlockSpec((1, w), lambda i: (0, 0))],
        out_specs=pl.BlockSpec((tm, w), row),
        out_shape=jax.ShapeDtypeStruct((n_rows, w), BF16),
        compiler_params=_cparams(("parallel",)),
        name="ssd_out",
    )(yf, yb, xbc, proj, jnp.repeat(ssd_d, SSD_HEAD_DIM).reshape(1, w), norm_g.reshape(1, w))


def _rms(x, g, n):
    return x * lax.rsqrt(jnp.sum(x * x, axis=-1, keepdims=True) * (1.0 / n) + EPS) * g


def _rope(x, cos, sin_signed, half):
    w = x.shape[1]
    lane = lax.broadcasted_iota(jnp.int32, x.shape, 1)
    partner = jnp.where(lane % (2 * half) < half, pltpu.roll(x, w - half, 1), pltpu.roll(x, half, 1))
    return x * cos + partner * sin_signed


def _mla_prep_kernel(cq_ref, ckv_ref, kr_ref, gcq_ref, wuq_ref, gckv_ref, wukn_ref, wuv_ref, gq_ref, gk_ref,
                     cos_ref, sin_ref, q_ref, k_ref, v_ref):
    hq = _rms(cq_ref[...].astype(F32), gcq_ref[...], MLA_Q_LORA).astype(BF16)
    q_all = jnp.dot(hq, wuq_ref[...], preferred_element_type=F32)
    hkv = _rms(ckv_ref[...].astype(F32), gckv_ref[...], MLA_KV_LORA).astype(BF16)
    kn_all = jnp.dot(hkv, wukn_ref[...], preferred_element_type=F32)
    v_ref[...] = jnp.dot(hkv, wuv_ref[...], preferred_element_type=F32).astype(v_ref.dtype)
    kr = kr_ref[...].astype(F32)
    lane = lax.broadcasted_iota(jnp.int32, kr.shape, 1)
    kr = jnp.where(lane < MLA_ROPE, kr, 0.0)
    cos, sin = cos_ref[...], sin_ref[...]
    w = MLA_QK_PAD
    n = MLA_NOPE

    def norm_rope(lo, hi, g):
        ss = jnp.sum(lo * lo, axis=-1, keepdims=True) + jnp.sum(hi * hi, axis=-1, keepdims=True)
        r = lax.rsqrt(ss * (1.0 / MLA_QK) + EPS)
        return lo * r * g[:, :n], _rope(hi * r * g[:, n:], cos, sin, MLA_ROPE // 4)

    for h in range(MLA_HEADS):
        lo, hi = norm_rope(q_all[:, h * w:h * w + n], q_all[:, h * w + n:(h + 1) * w], gq_ref[...])
        q_ref[:, h * w:h * w + n] = (lo * (MLA_SCALE * LOG2E)).astype(q_ref.dtype)
        q_ref[:, h * w + n:(h + 1) * w] = (hi * (MLA_SCALE * LOG2E)).astype(q_ref.dtype)
        lo, hi = norm_rope(kn_all[:, h * n:(h + 1) * n], kr, gk_ref[...])
        k_ref[:, h * w:h * w + n] = lo.astype(k_ref.dtype)
        k_ref[:, h * w + n:(h + 1) * w] = hi.astype(k_ref.dtype)


def _mla_prep(proj, p, cos, sin, n_rows, tm):
    hw = MLA_HEADS * MLA_QK_PAD
    hv = MLA_HEADS * MLA_V
    full = lambda i: (0, 0)
    return pl.pallas_call(
        _mla_prep_kernel,
        grid=(n_rows // tm,),
        in_specs=[pl.BlockSpec((tm, MLA_Q_LORA), lambda i: (i, PC_CQ // MLA_Q_LORA)),
                  pl.BlockSpec((tm, MLA_KV_LORA), lambda i: (i, PC_CKV // MLA_KV_LORA)),
                  pl.BlockSpec((tm, LANES), lambda i: (i, PC_KR // LANES)),
                  pl.BlockSpec((1, MLA_Q_LORA), full), pl.BlockSpec((MLA_Q_LORA, hw), full),
                  pl.BlockSpec((1, MLA_KV_LORA), full), pl.BlockSpec((MLA_KV_LORA, hv), full),
                  pl.BlockSpec((MLA_KV_LORA, hv), full),
                  pl.BlockSpec((1, MLA_QK_PAD), full), pl.BlockSpec((1, MLA_QK_PAD), full),
                  pl.BlockSpec((tm, LANES), lambda i: (i, 0)),
                  pl.BlockSpec((tm, LANES), lambda i: (i, 0))],
        out_specs=[pl.BlockSpec((tm, hw), lambda i: (i, 0)), pl.BlockSpec((tm, hw), lambda i: (i, 0)),
                   pl.BlockSpec((tm, hv), lambda i: (i, 0))],
        out_shape=[jax.ShapeDtypeStruct((n_rows, hw), BF16), jax.ShapeDtypeStruct((n_rows, hw), BF16),
                   jax.ShapeDtypeStruct((n_rows, hv), BF16)],
        compiler_params=_cparams(("parallel",)),
        name="mla_prep",
    )(proj, proj, proj, p['g_cq'], p['w_uq'], p['g_ckv'], p['w_ukn'], p['w_uv'], p['g_q'], p['g_k'], cos, sin)


def _diff_prep_kernel(q_in, k_in, gq_ref, gk_ref, cos_ref, sin_ref, q_ref, k_ref):
    cos, sin = cos_ref[...], sin_ref[...]
    w = DIFF_QK
    for c in range(2 * DIFF_HEADS):
        sl = slice(c * w, (c + 1) * w)
        qc = _rope(_rms(q_in[:, sl].astype(F32), gq_ref[...], w), cos, sin, w // 4)
        q_ref[:, sl] = (qc * (DIFF_SCALE * LOG2E)).astype(q_ref.dtype)
        kc = _rope(_rms(k_in[:, sl].astype(F32), gk_ref[...], w), cos, sin, w // 4)
        k_ref[:, sl] = kc.astype(k_ref.dtype)


def _diff_prep(proj, g_q, g_k, cos, sin, n_rows, tm):
    w = 2 * DIFF_HEADS * DIFF_QK
    full = lambda i: (0, 0)
    row = lambda i: (i, 0)
    return pl.pallas_call(
        _diff_prep_kernel,
        grid=(n_rows // tm,),
        in_specs=[pl.BlockSpec((tm, w), lambda i: (i, PC_DQ // w)),
                  pl.BlockSpec((tm, w), lambda i: (i, PC_DK // w)),
                  pl.BlockSpec((1, DIFF_QK), full), pl.BlockSpec((1, DIFF_QK), full),
                  pl.BlockSpec((tm, DIFF_QK), row), pl.BlockSpec((tm, DIFF_QK), row)],
        out_specs=[pl.BlockSpec((tm, w), row), pl.BlockSpec((tm, w), row)],
        out_shape=[jax.ShapeDtypeStruct((n_rows, w), BF16)] * 2,
        compiler_params=_cparams(("parallel",)),
        name="diff_prep",
    )(proj, proj, g_q.reshape(1, DIFF_QK), g_k.reshape(1, DIFF_QK), cos, sin)


def _scores(q, k):
    return lax.dot_general(q, k, (((1,), (1,)), ((), ())), preferred_element_type=F32)


def _online_update(s, v, m_ref, l_ref, acc_ref, c):
    tiles = [s[:, t * LANES:(t + 1) * LANES] for t in range(s.shape[1] // LANES)]
    m_old = m_ref[c]
    m_new = jnp.maximum(m_old, jnp.max(functools.reduce(jnp.maximum, tiles), axis=1, keepdims=True))
    alpha = jnp.exp2(m_old - m_new)
    m_ref[c] = m_new
    ps = [jnp.exp2(t - m_new) for t in tiles]
    if l_ref is None:
        v = jnp.concatenate([v, jnp.ones((v.shape[0], LANES), v.dtype)], axis=1)
    else:
        l_ref[c] = alpha * l_ref[c] + functools.reduce(jnp.add, ps)
    pv = jnp.dot(jnp.concatenate(ps, axis=1).astype(BF16), v, preferred_element_type=F32)
    reps = acc_ref.shape[-1] // LANES
    acc_ref[c] = acc_ref[c] * jnp.concatenate([alpha] * reps, axis=1) + pv


def _attn_kernel(*refs, n_comp, n_lat, tk, rowsum_on_mxu, finalize):
    q_ref, kc_ref, vc_ref = refs[:3]
    n_in = 3
    if n_lat:
        kl_ref, vl_ref = refs[3:5]
        n_in = 5
    n_scratch = 3 if rowsum_on_mxu else 4
    extras = refs[n_in:-n_scratch - 1]
    o_ref, s_ref, m_ref, acc_ref = refs[-n_scratch - 1], refs[-n_scratch], refs[-n_scratch + 1], refs[-1]
    l_ref = None if rowsum_on_mxu else refs[-2]
    dk = q_ref.shape[1] // n_comp
    vw = o_ref.shape[1]
    m_ref[...] = jnp.full(m_ref.shape, -jnp.inf, F32)
    if l_ref is not None:
        l_ref[...] = jnp.zeros(l_ref.shape, F32)
    acc_ref[...] = jnp.zeros(acc_ref.shape, F32)

    def chunk_scores(k, c):
        return _scores(q_ref[:, c * dk:(c + 1) * dk], k[:, c * dk:(c + 1) * dk])

    def put_scores(slot, j):
        k = kl_ref[pl.ds(pl.multiple_of(j * tk, tk), tk), :]
        for c in range(n_comp):
            s_ref[slot, c] = chunk_scores(k, c)

    def consume(slot, j):
        v = vl_ref[pl.ds(pl.multiple_of(j * tk, tk), tk), :]
        for c in range(n_comp):
            _online_update(s_ref[slot, c], v, m_ref, l_ref, acc_ref, c)

    kc = kc_ref[...]
    s_ctx = [chunk_scores(kc, c) for c in range(n_comp)]
    if n_lat:
        put_scores(0, 0)
    for c in range(n_comp):
        _online_update(s_ctx[c], vc_ref[...], m_ref, l_ref, acc_ref, c)
    if n_lat:
        def pair(i, last):
            put_scores(1, 2 * i + 1)
            consume(0, 2 * i)
            if not last:
                put_scores(0, 2 * i + 2)
            consume(1, 2 * i + 1)

        def body(i, carry):
            pair(i, False)
            return carry

        lax.fori_loop(0, n_lat // 2 - 1, body, 0)
        pair(n_lat // 2 - 1, True)
    if rowsum_on_mxu:
        outs = [acc_ref[c][:, :vw] / acc_ref[c][:, vw:] for c in range(n_comp)]
    else:
        outs = [acc_ref[c] / jnp.sum(l_ref[c], axis=1, keepdims=True) for c in range(n_comp)]
    o_ref[...] = finalize(outs, *extras).astype(o_ref.dtype)


def _mla_finalize(outs):
    return outs[0]


def _diff_finalize(outs, lam_ref, g_ref, *, lam_init):
    lv = lam_ref[...]
    lam = (jnp.exp(jnp.sum(lv[0:1, :] * lv[1:2, :], axis=1, keepdims=True))
           - jnp.exp(jnp.sum(lv[2:3, :] * lv[3:4, :], axis=1, keepdims=True)) + lam_init)
    o = outs[0] - lam * outs[1]
    return o * lax.rsqrt(jnp.mean(o * o, axis=-1, keepdims=True) + EPS) * g_ref[...] * (1.0 - lam_init)


def _attention(finalize, n_comp, q, k, v, extra, heads, qw, vw, v_col0, batch, seq, ctx_len, latent, tq, tk):
    lat_ctx_blocks = batch * seq // ctx_len
    ctx_spec = lambda width, c0: pl.BlockSpec((ctx_len, width), lambda b, h, i: (lat_ctx_blocks + b, c0 + h))
    if latent:
        q_spec = pl.BlockSpec((tq, qw), lambda b, h, i: (b * (seq // tq) + i, h))
        in_specs = [q_spec, ctx_spec(qw, 0), ctx_spec(vw, v_col0),
                    pl.BlockSpec((seq, qw), lambda b, h, i: (b, h)),
                    pl.BlockSpec((seq, vw), lambda b, h, i: (b, v_col0 + h))]
        args = [q, k, v, k, v]
        grid = (batch, heads, seq // tq)
        out_spec = pl.BlockSpec((tq, vw), lambda b, h, i: (b * (seq // tq) + i, h))
        out_rows = batch * seq
        n_lat = seq // tk
        assert n_lat >= 2 and n_lat % 2 == 0
    else:
        tq = tk = ctx_len
        in_specs = [ctx_spec(qw, 0), ctx_spec(qw, 0), ctx_spec(vw, v_col0)]
        args = [q, k, v]
        grid = (batch, heads, 1)
        out_spec = pl.BlockSpec((ctx_len, vw), lambda b, h, i: (b, h))
        out_rows = batch * ctx_len
        n_lat = 0
    for e in extra:
        in_specs.append(pl.BlockSpec(e.shape, lambda b, h, i: (0, 0)))
        args.append(e)
    rowsum_on_mxu = vw == LANES
    scratch = [pltpu.VMEM((2, n_comp, tq, tk), F32),
               pltpu.VMEM((n_comp, tq, LANES), F32)]
    if rowsum_on_mxu:
        scratch.append(pltpu.VMEM((n_comp, tq, vw + LANES), F32))
    else:
        scratch += [pltpu.VMEM((n_comp, tq, LANES), F32),
                    pltpu.VMEM((n_comp, tq, vw), F32)]
    return pl.pallas_call(
        functools.partial(_attn_kernel, n_comp=n_comp, n_lat=n_lat, tk=tk, rowsum_on_mxu=rowsum_on_mxu,
                          finalize=finalize),
        grid=grid,
        in_specs=in_specs,
        out_specs=out_spec,
        out_shape=jax.ShapeDtypeStruct((out_rows, heads * vw), BF16),
        scratch_shapes=scratch,
        compiler_params=_cparams(("parallel", "parallel", "arbitrary")),
        name="attn_latent" if latent else "attn_ctx",
    )(*args)


def _rope_tables(rows, rot_dim):
    axis_dim = rot_dim // 2
    inv = ROPE_THETA ** (-jnp.arange(0, axis_dim, 2, dtype=F32) / axis_dim)
    row = jnp.repeat(jnp.arange(rows, dtype=F32), GRID_W)
    colv = (jnp.arange(rows * GRID_W) % GRID_W).astype(F32)
    ar, ac = row[:, None] * inv, colv[:, None] * inv
    cos = jnp.concatenate([jnp.cos(ar), jnp.cos(ar), jnp.cos(ac), jnp.cos(ac)], axis=1)
    sin = jnp.concatenate([-jnp.sin(ar), jnp.sin(ar), -jnp.sin(ac), jnp.sin(ac)], axis=1)
    return cos, sin


def _token_tables(cos, sin, batch, ctx_rows, left, right):
    n = cos.shape[0]
    cos = jnp.concatenate([jnp.ones((n, left), F32), cos, jnp.ones((n, right), F32)], axis=1)
    sin = jnp.concatenate([jnp.zeros((n, left), F32), sin, jnp.zeros((n, right), F32)], axis=1)
    w = cos.shape[1]
    cos = jnp.concatenate([jnp.tile(cos, (batch, 1)), jnp.ones((ctx_rows, w), F32)], axis=0)
    sin = jnp.concatenate([jnp.tile(sin, (batch, 1)), jnp.zeros((ctx_rows, w), F32)], axis=0)
    return cos, sin


def _layout_w_in(w):
    d = w.shape[0]
    w = w.astype(BF16)
    s1, s2 = SSD_COLS, SSD_COLS + MLA_COLS
    xbc0 = SSD_INNER
    z = w[:, :SSD_INNER]
    xs = w[:, xbc0:xbc0 + SSD_INNER]
    bm = w[:, xbc0 + SSD_INNER:xbc0 + SSD_INNER + SSD_GN]
    cm = w[:, xbc0 + SSD_INNER + SSD_GN:xbc0 + SSD_CONV_CH]
    dt = w[:, xbc0 + SSD_CONV_CH:s1]
    cq = w[:, s1:s1 + MLA_Q_LORA]
    ckv = w[:, s1 + MLA_Q_LORA:s1 + MLA_Q_LORA + MLA_KV_LORA]
    kr = w[:, s1 + MLA_Q_LORA + MLA_KV_LORA:s2]
    qk = 2 * DIFF_HEADS * DIFF_QK
    dq, dk, dv = w[:, s2:s2 + qk], w[:, s2 + qk:s2 + 2 * qk], w[:, s2 + 2 * qk:]
    zeros = lambda n: jnp.zeros((d, n), w.dtype)
    main = jnp.concatenate([z, xs, dq, dk, dv, cq, bm, cm, ckv, kr, zeros(LANES - MLA_ROPE), zeros(LANES)], axis=1)
    dtw = jnp.concatenate([dt, zeros(LANES - 2 * SSD_HEADS)], axis=1)
    return main, dtw


def _layout_mla(w_uq, w_ukv, g_q, g_k):
    pad = MLA_QK_PAD - MLA_QK
    uq = jnp.pad(w_uq.reshape(MLA_Q_LORA, MLA_HEADS, MLA_QK), ((0, 0), (0, 0), (0, pad)))
    ukv = w_ukv.reshape(MLA_KV_LORA, MLA_HEADS, MLA_NOPE + MLA_V)
    return dict(
        w_uq=uq.reshape(MLA_Q_LORA, MLA_HEADS * MLA_QK_PAD).astype(BF16),
        w_ukn=ukv[:, :, :MLA_NOPE].reshape(MLA_KV_LORA, MLA_HEADS * MLA_NOPE).astype(BF16),
        w_uv=ukv[:, :, MLA_NOPE:].reshape(MLA_KV_LORA, MLA_HEADS * MLA_V).astype(BF16),
        g_q=jnp.pad(g_q, (0, pad)).reshape(1, MLA_QK_PAD),
        g_k=jnp.pad(g_k, (0, pad)).reshape(1, MLA_QK_PAD),
    )


def kernel(x, c, ctx, c_ctx, w_mod, b_mod, g_norm, w_ffn_gate, w_ffn_up, w_ffn_down, w_in, ssd_conv_w, ssd_conv_b, ssd_dt_bias, ssd_a_log, ssd_d, ssd_norm_g, mla_g_cq, mla_w_uq, mla_g_ckv, mla_w_ukv, mla_g_q, mla_g_k, diff_g_q, diff_g_k, diff_lambda, diff_subln_g, w_out):
    batch, seq, d = x.shape
    ctx_len = ctx.shape[1]
    depth = w_mod.shape[0]
    d_ff = w_ffn_gate.shape[-1]
    lat_rows, ctx_rows = batch * seq, batch * ctx_len
    n_rows = lat_rows + ctx_rows
    assert w_in.shape[-1] == IN_COLS and w_out.shape[1] == SSD_INNER + MLA_HEADS * MLA_V + DIFF_HEADS * DIFF_V
    assert seq % GRID_W == 0 and batch < MOD_ROWS
    tm = _pick(math.gcd(seq, ctx_rows), (512, 256, 128))
    tr = _pick(math.gcd(seq, ctx_len), (256, 128))
    assert seq % tm == 0 and ctx_rows % tm == 0 and seq % tr == 0 and ctx_len % tr == 0
    assert ctx_len % SSD_CHUNK == 0 and seq % SSD_CHUNK == 0 and lat_rows % ctx_len == 0
    tq = _pick(seq, (512, 256, 128))
    tk = _pick(seq // 2, (2048, 1024, 512, 256, 128))

    def group_of_row(r):
        return jnp.minimum(r // seq, batch)

    xs = jnp.concatenate([x.reshape(lat_rows, d), ctx.reshape(ctx_rows, d)], axis=0)
    c_rows = jnp.concatenate([c, c_ctx[None, :], jnp.zeros((MOD_ROWS - batch - 1, d), F32)], axis=0)
    mod = _modulation(c_rows, w_mod, b_mod).reshape(depth, MOD_ROWS, 1, N_MOD * d)

    rows = seq // GRID_W
    cos_m, sin_m = _token_tables(*_rope_tables(rows, MLA_ROPE), batch, ctx_rows, 0, LANES - MLA_ROPE)
    cos_d, sin_d = _token_tables(*_rope_tables(rows, DIFF_QK), batch, ctx_rows, 0, 0)

    tn_ff = _pick(d_ff, (1024, 512, 256, 128))
    tn_d = _pick(d, (512, 256, 128))
    tn_out = _pick(d, (1024, 512, 256, 128))
    tn_in = _pick(PROJ_COLS, (768,))

    w_gate, w_up, w_down = w_ffn_gate.astype(BF16), w_ffn_up.astype(BF16), w_ffn_down.astype(BF16)

    def ffn(xs, li, j, norm_idx, gate_idx, m_rows):
        h = _norm_mod(xs, g_norm[li, norm_idx], mod[li], norm_idx, m_rows, group_of_row, tr)
        act = _matmul_swiglu(h, w_gate, w_up, (li, j), m_rows, tm, tn_ff)
        return _matmul_resid([(act, w_down, (li, j))], xs, mod[li], gate_idx, 0.5, m_rows, group_of_row, tm, tn_d)

    for li in range(depth):
        need_ctx = li < depth - 1
        out_rows = n_rows if need_ctx else lat_rows
        lam_init = 0.8 - 0.6 * math.exp(-0.3 * li)

        xs = ffn(xs, li, 0, 0, 2, n_rows)

        h = _norm_mod(xs, g_norm[li, 1], mod[li], 1, n_rows, group_of_row, tr)
        w_main, w_dt = _layout_w_in(w_in[li])
        proj = _matmul(h, w_main, n_rows, BF16, tm, tn_in)
        dt_raw = _matmul(h, w_dt, n_rows, F32, tm, LANES)

        xbc = _ssd_conv(proj, ssd_conv_w[li], ssd_conv_b[li], n_rows, lat_rows, seq, ctx_len, tr)
        yf, yb = _ssd_scan(xbc, dt_raw, ssd_dt_bias[li], ssd_a_log[li], batch, seq, ctx_len)
        ssd_o = _ssd_out(yf, yb, xbc, proj, ssd_d[li], ssd_norm_g[li], out_rows, tr)

        mla_p = _layout_mla(mla_w_uq[li], mla_w_ukv[li], mla_g_q[li], mla_g_k[li])
        mla_p['g_cq'] = mla_g_cq[li].reshape(1, MLA_Q_LORA)
        mla_p['g_ckv'] = mla_g_ckv[li].reshape(1, MLA_KV_LORA)
        mq, mk, mv = _mla_prep(proj, mla_p, cos_m, sin_m, n_rows, tr)
        mla_args = (_mla_finalize, 1, mq, mk, mv, [], MLA_HEADS, MLA_QK_PAD, MLA_V, 0, batch, seq, ctx_len)
        mla_o = _attention(*mla_args, True, tq, tk)

        dq, dk = _diff_prep(proj, diff_g_q[li], diff_g_k[li], cos_d, sin_d, n_rows, tr)
        diff_fin = functools.partial(_diff_finalize, lam_init=lam_init)
        extra = [diff_lambda[li], diff_subln_g[li].reshape(1, DIFF_V)]
        diff_args = (diff_fin, 2, dq, dk, proj, extra, DIFF_HEADS, 2 * DIFF_QK, DIFF_V, PC_DV // DIFF_V,
                     batch, seq, ctx_len)
        diff_o = _attention(*diff_args, True, tq, tk)
        if need_ctx:
            mla_o = jnp.concatenate([mla_o, _attention(*mla_args, False, tq, tk)], axis=0)
            diff_o = jnp.concatenate([diff_o, _attention(*diff_args, False, tq, tk)], axis=0)

        wo = w_out[li].astype(BF16)
        o1, o2 = SSD_INNER, SSD_INNER + MLA_HEADS * MLA_V
        xs = _matmul_resid([(ssd_o, wo[:o1], ()), (mla_o, wo[o1:o2], ()), (diff_o, wo[o2:], ())], xs, mod[li], 5,
                           1.0, out_rows, group_of_row, tm, tn_out)

        xs = ffn(xs, li, 1, 2, 8, out_rows)

    return xs[:lat_rows].reshape(batch, seq, d)
```

```python
import functools
import math

import jax
import jax.numpy as jnp
from jax import lax
from jax.experimental import pallas as pl
from jax.experimental.pallas import tpu as pltpu

F32 = jnp.float32
BF16 = jnp.bfloat16

GRID_W = 64
EPS = 1e-6
ROPE_THETA = 10000.0
N_MOD = 9
SSD_INNER = 2048
SSD_HEAD_DIM = 64
SSD_HEADS = 32
SSD_GROUPS = 4
SSD_STATE = 128
SSD_CONV = 5
SSD_CHUNK = 128
SSD_GN = SSD_GROUPS * SSD_STATE
SSD_CONV_CH = SSD_INNER + 2 * SSD_GN
SSD_COLS = 2 * SSD_INNER + 2 * SSD_GN + 2 * SSD_HEADS
MLA_HEADS = 8
MLA_NOPE = 128
MLA_ROPE = 64
MLA_V = 128
MLA_Q_LORA = 1024
MLA_KV_LORA = 512
MLA_QK = MLA_NOPE + MLA_ROPE
MLA_QK_PAD = 256
MLA_SCALE = 1.0 / math.sqrt(MLA_QK)
LOG2E = math.log2(math.e)
MLA_COLS = MLA_Q_LORA + MLA_KV_LORA + MLA_ROPE
DIFF_HEADS = 4
DIFF_QK = 128
DIFF_V = 2 * DIFF_QK
DIFF_SCALE = 1.0 / math.sqrt(DIFF_QK)
DIFF_COLS = DIFF_HEADS * (4 * DIFF_QK + DIFF_V)
IN_COLS = SSD_COLS + MLA_COLS + DIFF_COLS

LANES = 128
BF16_SUBLANES = 16
MOD_ROWS = 8
VMEM_LIMIT = 56 * 1024 * 1024

PC_Z = 0
PC_XS = 2048
PC_DQ = 4096
PC_DK = 5120
PC_DV = 6144
PC_CQ = 7168
PC_B = 8192
PC_C = 8704
PC_CKV = 9216
PC_KR = 9728
PC_PAD = 9856
PROJ_COLS = 9984


def _cparams(sem):
    return pltpu.CompilerParams(dimension_semantics=sem, vmem_limit_bytes=VMEM_LIMIT)


def _pick(n, prefs):
    for p in prefs:
        if n % p == 0:
            return p
    return n


def _silu(x):
    return x / (1.0 + jnp.exp(-x))


def _weight_spec(w, lead, tn, single_buffer=False):
    mode = dict(pipeline_mode=pl.Buffered(1)) if single_buffer else {}
    return pl.BlockSpec((None,) * len(lead) + (w.shape[-2], tn), lambda j, i: tuple(lead) + (0, j), **mode)


def _mm_kernel(a_ref, b_ref, o_ref):
    o_ref[...] = jnp.dot(a_ref[...], b_ref[...], preferred_element_type=F32).astype(o_ref.dtype)


def _matmul(a, b, m_rows, out_dtype, tm, tn):
    k = a.shape[1]
    n = b.shape[1]
    return pl.pallas_call(
        _mm_kernel,
        grid=(n // tn, m_rows // tm),
        in_specs=[pl.BlockSpec((tm, k), lambda j, i: (i, 0)),
                  pl.BlockSpec((k, tn), lambda j, i: (0, j))],
        out_specs=pl.BlockSpec((tm, tn), lambda j, i: (i, j)),
        out_shape=jax.ShapeDtypeStruct((m_rows, n), out_dtype),
        compiler_params=_cparams(("parallel", "parallel")),
        name="mm_plain",
    )(a, b)


def _swiglu_kernel(a_ref, wg_ref, wu_ref, o_ref):
    a = a_ref[...]
    g = jnp.dot(a, wg_ref[...], preferred_element_type=F32)
    u = jnp.dot(a, wu_ref[...], preferred_element_type=F32)
    o_ref[...] = (_silu(g) * u).astype(o_ref.dtype)


def _matmul_swiglu(a, wg, wu, lead, m_rows, tm, tn):
    k = a.shape[1]
    n = wg.shape[-1]
    return pl.pallas_call(
        _swiglu_kernel,
        grid=(n // tn, m_rows // tm),
        in_specs=[pl.BlockSpec((tm, k), lambda j, i: (i, 0)),
                  _weight_spec(wg, lead, tn), _weight_spec(wu, lead, tn)],
        out_specs=pl.BlockSpec((tm, tn), lambda j, i: (i, j)),
        out_shape=jax.ShapeDtypeStruct((m_rows, n), BF16),
        compiler_params=_cparams(("parallel", "parallel")),
        name="mm_swiglu",
    )(a, wg, wu)


def _row_source_specs(parts, t, width, col):
    if len(parts) == 1:
        return [pl.BlockSpec((t, width), lambda *ids: (ids[-1], col(*ids)))]
    n0 = parts[0].shape[0] // t
    return [pl.BlockSpec((t, width), lambda *ids: (jnp.minimum(ids[-1], n0 - 1), col(*ids))),
            pl.BlockSpec((t, width), lambda *ids: (jnp.maximum(ids[-1] - n0, 0), col(*ids)))]


def _resid_kernel(*refs, n_pairs, n_src, n_first, scale):
    x_refs = refs[2 * n_pairs:2 * n_pairs + n_src]
    gate_ref, o_ref = refs[2 * n_pairs + n_src], refs[2 * n_pairs + n_src + 1]
    acc = jnp.dot(refs[0][...], refs[1][...], preferred_element_type=F32)
    for p in range(1, n_pairs):
        acc = acc + jnp.dot(refs[2 * p][...], refs[2 * p + 1][...], preferred_element_type=F32)
    x = x_refs[0][...]
    if n_src == 2:
        x = jnp.where(pl.program_id(1) < n_first, x, x_refs[1][...])
    o_ref[...] = x + (scale * gate_ref[...]) * acc


def _matmul_resid(pairs, x, mod4, gate_idx, scale, m_rows, group_of_row, tm, tn, single_buffer=False):
    d = x[0].shape[1]
    in_specs = []
    args = []
    for a, b, lead in pairs:
        in_specs.append(pl.BlockSpec((tm, a.shape[1]), lambda j, i: (i, 0)))
        in_specs.append(_weight_spec(b, lead, tn, single_buffer))
        args += [a, b]
    in_specs += _row_source_specs(x, tm, tn, lambda j, i: j)
    gate_blk = gate_idx * (d // tn)
    in_specs.append(pl.BlockSpec((None, 1, tn), lambda j, i: (group_of_row(i * tm), 0, gate_blk + j)))
    args += [*x, mod4]
    return pl.pallas_call(
        functools.partial(_resid_kernel, n_pairs=len(pairs), n_src=len(x), n_first=x[0].shape[0] // tm,
                          scale=scale),
        grid=(d // tn, m_rows // tm),
        in_specs=in_specs,
        out_specs=pl.BlockSpec((tm, tn), lambda j, i: (i, j)),
        out_shape=jax.ShapeDtypeStruct((m_rows, d), F32),
        compiler_params=_cparams(("parallel", "parallel")),
        name="mm_resid",
    )(*args)


def _mod_kernel(c_ref, w_ref, b_ref, o_ref):
    s = _silu(c_ref[...]).astype(BF16)
    w = w_ref[...].astype(BF16)
    o_ref[...] = jnp.dot(s, w, preferred_element_type=F32) + b_ref[...]


def _modulation(c_rows, w_mod, b_mod):
    depth, d, n = w_mod.shape
    tn = _pick(n, (512, 256, 128))
    return pl.pallas_call(
        _mod_kernel,
        grid=(depth, n // tn),
        in_specs=[pl.BlockSpec((MOD_ROWS, d), lambda l, j: (0, 0)),
                  pl.BlockSpec((None, d, tn), lambda l, j: (l, 0, j)),
                  pl.BlockSpec((None, 1, tn), lambda l, j: (l, 0, j))],
        out_specs=pl.BlockSpec((None, MOD_ROWS, tn), lambda l, j: (l, 0, j)),
        out_shape=jax.ShapeDtypeStruct((depth, MOD_ROWS, n), F32),
        compiler_params=_cparams(("parallel", "parallel")),
        name="modulation",
    )(c_rows, w_mod, b_mod.reshape(depth, 1, n))


def _norm_mod_kernel(*refs, n_first):
    x_refs = refs[:-4]
    g_ref, shift_ref, scale_ref, o_ref = refs[-4:]
    first = pl.program_id(0) < n_first
    gain = g_ref[...] * (1.0 + scale_ref[...])
    shift = shift_ref[...]
    rows = BF16_SUBLANES

    def body(i, carry):
        r = pl.multiple_of(i * rows, rows)
        x = x_refs[0][pl.ds(r, rows), :]
        if len(x_refs) == 2:
            x = jnp.where(first, x, x_refs[1][pl.ds(r, rows), :])
        y = x * lax.rsqrt(jnp.mean(x * x, axis=-1, keepdims=True) + EPS)
        o_ref[pl.ds(r, rows), :] = (y * gain + shift).astype(o_ref.dtype)
        return carry

    lax.fori_loop(0, o_ref.shape[0] // rows, body, 0, unroll=2)


def _norm_mod(x, g, mod4, idx, m_rows, group_of_row, tm):
    d = x[0].shape[1]
    return pl.pallas_call(
        functools.partial(_norm_mod_kernel, n_first=x[0].shape[0] // tm),
        grid=(m_rows // tm,),
        in_specs=_row_source_specs(x, tm, d, lambda i: 0) + [
                  pl.BlockSpec((1, d), lambda i: (0, 0)),
                  pl.BlockSpec((None, 1, d), lambda i: (group_of_row(i * tm), 0, 3 * idx)),
                  pl.BlockSpec((None, 1, d), lambda i: (group_of_row(i * tm), 0, 3 * idx + 1))],
        out_specs=pl.BlockSpec((tm, d), lambda i: (i, 0)),
        out_shape=jax.ShapeDtypeStruct((m_rows, d), BF16),
        compiler_params=_cparams(("parallel",)),
        name="norm_mod",
    )(*x, g.reshape(1, d), mod4, mod4)


def _conv_kernel(prev_ref, cur_ref, next_ref, w_ref, b_ref, o_ref, ext_ref, *, tr, lat_rows, seq, ctx_len):
    r0 = pl.program_id(0) * tr
    in_lat = r0 < lat_rows
    off = jnp.where(in_lat, r0 % seq, (r0 - lat_rows) % ctx_len)
    seg = jnp.where(in_lat, seq, ctx_len)
    first = off == 0
    last = off + tr == seg
    h = BF16_SUBLANES
    ext_ref[0:h, :] = jnp.where(first, 0.0, prev_ref[...].astype(F32))
    ext_ref[h:h + tr, :] = cur_ref[...].astype(F32)
    ext_ref[h + tr:2 * h + tr, :] = jnp.where(last, 0.0, next_ref[...].astype(F32))
    acc = b_ref[...] + w_ref[0:1, :] * ext_ref[pl.ds(h - 2, tr), :]
    for k in range(1, SSD_CONV):
        acc = acc + w_ref[k:k + 1, :] * ext_ref[pl.ds(h - 2 + k, tr), :]
    o_ref[...] = _silu(acc).astype(o_ref.dtype)


def _ssd_conv(proj, conv_w, conv_b, n_rows, lat_rows, seq, ctx_len, tr):
    tc = 1024
    h = BF16_SUBLANES
    rb = tr // h
    n_halo = n_rows // h
    xs_blk, bc_blk = PC_XS // tc, PC_B // tc
    n_xs = SSD_INNER // tc

    def col(j):
        return jnp.where(j < n_xs, xs_blk + j, bc_blk + (j - n_xs))

    return pl.pallas_call(
        functools.partial(_conv_kernel, tr=tr, lat_rows=lat_rows, seq=seq, ctx_len=ctx_len),
        grid=(n_rows // tr, SSD_CONV_CH // tc),
        in_specs=[pl.BlockSpec((h, tc), lambda i, j: (jnp.maximum(i * rb - 1, 0), col(j))),
                  pl.BlockSpec((tr, tc), lambda i, j: (i, col(j))),
                  pl.BlockSpec((h, tc), lambda i, j: (jnp.minimum((i + 1) * rb, n_halo - 1), col(j))),
                  pl.BlockSpec((SSD_CONV, tc), lambda i, j: (0, j)),
                  pl.BlockSpec((1, tc), lambda i, j: (0, j))],
        out_specs=pl.BlockSpec((tr, tc), lambda i, j: (i, j)),
        out_shape=jax.ShapeDtypeStruct((n_rows, SSD_CONV_CH), BF16),
        scratch_shapes=[pltpu.VMEM((tr + 2 * h, tc), F32)],
        compiler_params=_cparams(("parallel", "parallel")),
        name="ssd_conv",
    )(proj, proj, proj, conv_w, conv_b.reshape(1, SSD_CONV_CH))


N_PAIRS = SSD_HEADS // 2
PAIRS_PER_GROUP = N_PAIRS // SSD_GROUPS


def _ssd_direction(d, x_ref, b_ref, c_ref, dt_ref, bias_ref, alog_ref, y_ref, st_ref,
                   cs_ref, cst_ref, dtt_ref, woutt_ref, g_ref, bt_ref, yoff_ref, xp_ref, yp_ref):
    q = SSD_CHUNK
    lane = lax.broadcasted_iota(jnp.int32, (q, LANES), 1)
    row = lax.broadcasted_iota(jnp.int32, (q, q), 0)
    col = lax.broadcasted_iota(jnp.int32, (q, q), 1)
    mask = (row >= col) if d == 0 else (row <= col)

    z = dt_ref[...] + bias_ref[...]
    dt = jnp.maximum(z, 0.0) + jnp.log1p(jnp.exp(-jnp.abs(z)))
    a_dt = dt * (-jnp.exp(alog_ref[...]))
    hi = a_dt.astype(BF16)
    r1 = a_dt - hi.astype(F32)
    mid = r1.astype(BF16)
    lo = (r1 - mid.astype(F32)).astype(BF16)
    tri = jnp.where(mask, 1.0, 0.0).astype(BF16)
    c3 = jnp.dot(tri, jnp.concatenate([hi, mid, lo], axis=1), preferred_element_type=F32)
    cs = c3[:, 0:LANES] + c3[:, LANES:2 * LANES] + c3[:, 2 * LANES:3 * LANES]
    end = q - 1 if d == 0 else 0
    a_tot = cs[end:end + 1, :]
    cs_ref[...] = cs
    cst_ref[...] = cs.T
    dtt_ref[...] = dt.T
    woutt_ref[...] = (dt * jnp.exp(a_tot - cs)).T

    for g in range(SSD_GROUPS):
        bg = b_ref[:, g * SSD_STATE:(g + 1) * SSD_STATE]
        cg = c_ref[:, g * SSD_STATE:(g + 1) * SSD_STATE]
        g_ref[g] = lax.dot_general(cg, bg, (((1,), (1,)), ((), ())), preferred_element_type=F32)
        bt_ref[g] = bg.astype(F32).T
        for pp in range(PAIRS_PER_GROUP):
            p = g * PAIRS_PER_GROUP + pp
            yoff_ref[p] = jnp.dot(cg, st_ref[p].astype(BF16), preferred_element_type=F32)
    for p in range(N_PAIRS):
        xp_ref[p] = x_ref[:, p * LANES:(p + 1) * LANES]

    half = lane < SSD_HEAD_DIM

    def pair_body(p, carry):
        g = p // PAIRS_PER_GROUP
        h0 = 2 * p + SSD_HEADS * d
        gmat = g_ref[g]
        bt = bt_ref[g]
        xp = xp_ref[p]
        ys, ups, dins = [], [], []
        for hh in (h0, h0 + 1):
            cs_col = jnp.sum(jnp.where(lane == hh, cs_ref[...], 0.0), axis=1, keepdims=True)
            lmat = jnp.exp(jnp.where(mask, cs_col - cst_ref[pl.ds(hh, 1), :], -1e30))
            m = (gmat * lmat * dtt_ref[pl.ds(hh, 1), :]).astype(BF16)
            ys.append(jnp.dot(m, xp, preferred_element_type=F32))
            bw = (bt * woutt_ref[pl.ds(hh, 1), :]).astype(BF16)
            ups.append(jnp.dot(bw, xp, preferred_element_type=F32))
            dins.append(jnp.exp(cs_col))
        din = jnp.where(half, dins[0], dins[1])
        yp_ref[p] = jnp.where(half, ys[0], ys[1]) + yoff_ref[p] * din
        st_ref[p] = st_ref[p] * din[end:end + 1, :] + jnp.where(half, ups[0], ups[1])
        return carry

    lax.fori_loop(0, N_PAIRS, pair_body, 0, unroll=4)
    for p in range(N_PAIRS):
        y_ref[:, p * LANES:(p + 1) * LANES] = yp_ref[p]


def _ssd_kernel(xf_ref, bf_ref, cf_ref, dtf_ref, xb_ref, bb_ref, cb_ref, dtb_ref, bias_ref, alog_ref,
                yf_ref, yb_ref, stf_ref, stb_ref, cs_ref, cst_ref, dtt_ref, woutt_ref,
                g_ref, bt_ref, yoff_ref, xp_ref, yp_ref):
    @pl.when(pl.program_id(1) == 0)
    def _():
        stf_ref[...] = jnp.zeros_like(stf_ref)
        stb_ref[...] = jnp.zeros_like(stb_ref)

    shared = (cs_ref, cst_ref, dtt_ref, woutt_ref, g_ref, bt_ref, yoff_ref, xp_ref, yp_ref)
    _ssd_direction(0, xf_ref, bf_ref, cf_ref, dtf_ref, bias_ref, alog_ref, yf_ref, stf_ref, *shared)
    _ssd_direction(1, xb_ref, bb_ref, cb_ref, dtb_ref, bias_ref, alog_ref, yb_ref, stb_ref, *shared)


def _ssd_scan(xbc, dt_raw, dt_bias, a_log, batch, seq, ctx_len):
    q = SSD_CHUNK
    n_rows = xbc.shape[0]
    ncc, nlc = ctx_len // q, seq // q
    lat_blocks = batch * seq // q

    def fwd_blk(b, s):
        return jnp.where(s < ncc, lat_blocks + b * ncc + s, b * nlc + (s - ncc))

    def bwd_blk(b, s):
        return jnp.where(s < ncc, lat_blocks + b * ncc + (ncc - 1 - s), b * nlc + (nlc - 1 - (s - ncc)))

    def specs(blk):
        return [pl.BlockSpec((q, SSD_INNER), lambda b, s: (blk(b, s), 0)),
                pl.BlockSpec((q, SSD_GN), lambda b, s: (blk(b, s), SSD_INNER // SSD_GN)),
                pl.BlockSpec((q, SSD_GN), lambda b, s: (blk(b, s), SSD_INNER // SSD_GN + 1)),
                pl.BlockSpec((q, LANES), lambda b, s: (blk(b, s), 0))]

    pad = jnp.zeros((LANES - 2 * SSD_HEADS,), F32)
    bias = jnp.concatenate([dt_bias.reshape(-1), pad]).reshape(1, LANES)
    alog = jnp.concatenate([a_log.reshape(-1), pad]).reshape(1, LANES)
    tile = (q, LANES)
    return pl.pallas_call(
        _ssd_kernel,
        grid=(batch, ncc + nlc),
        in_specs=specs(fwd_blk) + specs(bwd_blk) + [pl.BlockSpec((1, LANES), lambda b, s: (0, 0))] * 2,
        out_specs=[pl.BlockSpec((q, SSD_INNER), lambda b, s: (fwd_blk(b, s), 0)),
                   pl.BlockSpec((q, SSD_INNER), lambda b, s: (bwd_blk(b, s), 0))],
        out_shape=[jax.ShapeDtypeStruct((n_rows, SSD_INNER), F32)] * 2,
        scratch_shapes=[pltpu.VMEM((N_PAIRS, SSD_STATE, LANES), F32),
                        pltpu.VMEM((N_PAIRS, SSD_STATE, LANES), F32),
                        pltpu.VMEM(tile, F32), pltpu.VMEM(tile, F32), pltpu.VMEM(tile, F32),
                        pltpu.VMEM(tile, F32),
                        pltpu.VMEM((SSD_GROUPS, q, q), F32),
                        pltpu.VMEM((SSD_GROUPS, SSD_STATE, q), F32),
                        pltpu.VMEM((N_PAIRS, q, LANES), F32),
                        pltpu.VMEM((N_PAIRS, q, LANES), BF16),
                        pltpu.VMEM((N_PAIRS, q, LANES), F32)],
        compiler_params=_cparams(("parallel", "arbitrary")),
        name="ssd_scan",
    )(xbc, xbc, xbc, dt_raw, xbc, xbc, xbc, dt_raw, bias, alog)


def _ssd_out_kernel(yf_ref, yb_ref, x_ref, z_ref, d_ref, g_ref, o_ref):
    z = z_ref[...].astype(F32)
    y = (yf_ref[...] + yb_ref[...] + d_ref[...] * x_ref[...].astype(F32)) * _silu(z)
    y = y * lax.rsqrt(jnp.mean(y * y, axis=-1, keepdims=True) + EPS) * g_ref[...]
    o_ref[...] = y.astype(o_ref.dtype)


def _ssd_out(yf, yb, xbc, proj, ssd_d, norm_g, n_rows, tm):
    w = SSD_INNER
    row = lambda i: (i, 0)
    return pl.pallas_call(
        _ssd_out_kernel,
        grid=(n_rows // tm,),
        in_specs=[pl.BlockSpec((tm, w), row), pl.BlockSpec((tm, w), row), pl.BlockSpec((tm, w), row),
                  pl.BlockSpec((tm, w), lambda i: (i, PC_Z // w)),
                  pl.BlockSpec((1, w), lambda i: (0, 0)), pl.BlockSpec((1, w), lambda i: (0, 0))],
        out_specs=pl.BlockSpec((tm, w), row),
        out_shape=jax.ShapeDtypeStruct((n_rows, w), BF16),
        compiler_params=_cparams(("parallel",)),
        name="ssd_out",
    )(yf, yb, xbc, proj, jnp.repeat(ssd_d, SSD_HEAD_DIM).reshape(1, w), norm_g.reshape(1, w))


def _rms(x, g, n):
    return x * lax.rsqrt(jnp.sum(x * x, axis=-1, keepdims=True) * (1.0 / n) + EPS) * g


def _rope(x, cos, sin_signed, half):
    w = x.shape[1]
    lane = lax.broadcasted_iota(jnp.int32, x.shape, 1)
    partner = jnp.where(lane % (2 * half) < half, pltpu.roll(x, w - half, 1), pltpu.roll(x, half, 1))
    return x * cos + partner * sin_signed


def _mla_prep_kernel(cq_ref, ckv_ref, kr_ref, gcq_ref, wuq_ref, gckv_ref, wukn_ref, wuv_ref, gq_ref, gk_ref,
                     cos_ref, sin_ref, q_ref, k_ref, v_ref):
    hq = _rms(cq_ref[...].astype(F32), gcq_ref[...], MLA_Q_LORA).astype(BF16)
    q_all = jnp.dot(hq, wuq_ref[...], preferred_element_type=F32)
    hkv = _rms(ckv_ref[...].astype(F32), gckv_ref[...], MLA_KV_LORA).astype(BF16)
    kn_all = jnp.dot(hkv, wukn_ref[...], preferred_element_type=F32)
    v_ref[...] = jnp.dot(hkv, wuv_ref[...], preferred_element_type=F32).astype(v_ref.dtype)
    kr = kr_ref[...].astype(F32)
    lane = lax.broadcasted_iota(jnp.int32, kr.shape, 1)
    kr = jnp.where(lane < MLA_ROPE, kr, 0.0)
    cos, sin = cos_ref[...], sin_ref[...]
    w = MLA_QK_PAD
    n = MLA_NOPE

    def norm_rope(lo, hi, g):
        ss = jnp.sum(lo * lo, axis=-1, keepdims=True) + jnp.sum(hi * hi, axis=-1, keepdims=True)
        r = lax.rsqrt(ss * (1.0 / MLA_QK) + EPS)
        return lo * r * g[:, :n], _rope(hi * r * g[:, n:], cos, sin, MLA_ROPE // 4)

    for h in range(MLA_HEADS):
        lo, hi = norm_rope(q_all[:, h * w:h * w + n], q_all[:, h * w + n:(h + 1) * w], gq_ref[...])
        q_ref[:, h * w:h * w + n] = (lo * (MLA_SCALE * LOG2E)).astype(q_ref.dtype)
        q_ref[:, h * w + n:(h + 1) * w] = (hi * (MLA_SCALE * LOG2E)).astype(q_ref.dtype)
        lo, hi = norm_rope(kn_all[:, h * n:(h + 1) * n], kr, gk_ref[...])
        k_ref[:, h * w:h * w + n] = lo.astype(k_ref.dtype)
        k_ref[:, h * w + n:(h + 1) * w] = hi.astype(k_ref.dtype)


def _mla_prep(proj, p, cos, sin, n_rows, tm):
    hw = MLA_HEADS * MLA_QK_PAD
    hv = MLA_HEADS * MLA_V
    full = lambda i: (0, 0)
    return pl.pallas_call(
        _mla_prep_kernel,
        grid=(n_rows // tm,),
        in_specs=[pl.BlockSpec((tm, MLA_Q_LORA), lambda i: (i, PC_CQ // MLA_Q_LORA)),
                  pl.BlockSpec((tm, MLA_KV_LORA), lambda i: (i, PC_CKV // MLA_KV_LORA)),
                  pl.BlockSpec((tm, LANES), lambda i: (i, PC_KR // LANES)),
                  pl.BlockSpec((1, MLA_Q_LORA), full), pl.BlockSpec((MLA_Q_LORA, hw), full),
                  pl.BlockSpec((1, MLA_KV_LORA), full), pl.BlockSpec((MLA_KV_LORA, hv), full),
                  pl.BlockSpec((MLA_KV_LORA, hv), full),
                  pl.BlockSpec((1, MLA_QK_PAD), full), pl.BlockSpec((1, MLA_QK_PAD), full),
                  pl.BlockSpec((tm, LANES), lambda i: (i, 0)),
                  pl.BlockSpec((tm, LANES), lambda i: (i, 0))],
        out_specs=[pl.BlockSpec((tm, hw), lambda i: (i, 0)), pl.BlockSpec((tm, hw), lambda i: (i, 0)),
                   pl.BlockSpec((tm, hv), lambda i: (i, 0))],
        out_shape=[jax.ShapeDtypeStruct((n_rows, hw), BF16), jax.ShapeDtypeStruct((n_rows, hw), BF16),
                   jax.ShapeDtypeStruct((n_rows, hv), BF16)],
        compiler_params=_cparams(("parallel",)),
        name="mla_prep",
    )(proj, proj, proj, p['g_cq'], p['w_uq'], p['g_ckv'], p['w_ukn'], p['w_uv'], p['g_q'], p['g_k'], cos, sin)


def _diff_prep_kernel(q_in, k_in, gq_ref, gk_ref, cos_ref, sin_ref, q_ref, k_ref):
    cos, sin = cos_ref[...], sin_ref[...]
    w = DIFF_QK
    for c in range(2 * DIFF_HEADS):
        sl = slice(c * w, (c + 1) * w)
        qc = _rope(_rms(q_in[:, sl].astype(F32), gq_ref[...], w), cos, sin, w // 4)
        q_ref[:, sl] = (qc * (DIFF_SCALE * LOG2E)).astype(q_ref.dtype)
        kc = _rope(_rms(k_in[:, sl].astype(F32), gk_ref[...], w), cos, sin, w // 4)
        k_ref[:, sl] = kc.astype(k_ref.dtype)


def _diff_prep(proj, g_q, g_k, cos, sin, n_rows, tm):
    w = 2 * DIFF_HEADS * DIFF_QK
    full = lambda i: (0, 0)
    row = lambda i: (i, 0)
    return pl.pallas_call(
        _diff_prep_kernel,
        grid=(n_rows // tm,),
        in_specs=[pl.BlockSpec((tm, w), lambda i: (i, PC_DQ // w)),
                  pl.BlockSpec((tm, w), lambda i: (i, PC_DK // w)),
                  pl.BlockSpec((1, DIFF_QK), full), pl.BlockSpec((1, DIFF_QK), full),
                  pl.BlockSpec((tm, DIFF_QK), row), pl.BlockSpec((tm, DIFF_QK), row)],
        out_specs=[pl.BlockSpec((tm, w), row), pl.BlockSpec((tm, w), row)],
        out_shape=[jax.ShapeDtypeStruct((n_rows, w), BF16)] * 2,
        compiler_params=_cparams(("parallel",)),
        name="diff_prep",
    )(proj, proj, g_q.reshape(1, DIFF_QK), g_k.reshape(1, DIFF_QK), cos, sin)


def _scores(q, k):
    return lax.dot_general(q, k, (((1,), (1,)), ((), ())), preferred_element_type=F32)


def _online_update(s, v, m_ref, l_ref, acc_ref, c):
    tiles = [s[:, t * LANES:(t + 1) * LANES] for t in range(s.shape[1] // LANES)]
    m_old = m_ref[c]
    m_new = jnp.maximum(m_old, jnp.max(functools.reduce(jnp.maximum, tiles), axis=1, keepdims=True))
    alpha = jnp.exp2(m_old - m_new)
    m_ref[c] = m_new
    ps = [jnp.exp2(t - m_new) for t in tiles]
    if l_ref is None:
        v = jnp.concatenate([v, jnp.ones((v.shape[0], LANES), v.dtype)], axis=1)
    else:
        l_ref[c] = alpha * l_ref[c] + functools.reduce(jnp.add, ps)
    pv = jnp.dot(jnp.concatenate(ps, axis=1).astype(BF16), v, preferred_element_type=F32)
    reps = acc_ref.shape[-1] // LANES
    acc_ref[c] = acc_ref[c] * jnp.concatenate([alpha] * reps, axis=1) + pv


def _attn_kernel(*refs, n_comp, n_lat, tk, rowsum_on_mxu, finalize):
    q_ref, kc_ref, vc_ref = refs[:3]
    n_in = 3
    if n_lat:
        kl_ref, vl_ref = refs[3:5]
        n_in = 5
    n_scratch = 3 if rowsum_on_mxu else 4
    extras = refs[n_in:-n_scratch - 1]
    o_ref, s_ref, m_ref, acc_ref = refs[-n_scratch - 1], refs[-n_scratch], refs[-n_scratch + 1], refs[-1]
    l_ref = None if rowsum_on_mxu else refs[-2]
    dk = q_ref.shape[1] // n_comp
    vw = o_ref.shape[1]
    m_ref[...] = jnp.full(m_ref.shape, -jnp.inf, F32)
    if l_ref is not None:
        l_ref[...] = jnp.zeros(l_ref.shape, F32)
    acc_ref[...] = jnp.zeros(acc_ref.shape, F32)

    def chunk_scores(k, c):
        return _scores(q_ref[:, c * dk:(c + 1) * dk], k[:, c * dk:(c + 1) * dk])

    def put_scores(slot, j):
        k = kl_ref[pl.ds(pl.multiple_of(j * tk, tk), tk), :]
        for c in range(n_comp):
            s_ref[slot, c] = chunk_scores(k, c)

    def consume(slot, j):
        v = vl_ref[pl.ds(pl.multiple_of(j * tk, tk), tk), :]
        for c in range(n_comp):
            _online_update(s_ref[slot, c], v, m_ref, l_ref, acc_ref, c)

    kc = kc_ref[...]
    s_ctx = [chunk_scores(kc, c) for c in range(n_comp)]
    if n_lat:
        put_scores(0, 0)
    for c in range(n_comp):
        _online_update(s_ctx[c], vc_ref[...], m_ref, l_ref, acc_ref, c)
    if n_lat:
        def pair(i, last):
            put_scores(1, 2 * i + 1)
            consume(0, 2 * i)
            if not last:
                put_scores(0, 2 * i + 2)
            consume(1, 2 * i + 1)

        def body(i, carry):
            pair(i, False)
            return carry

        lax.fori_loop(0, n_lat // 2 - 1, body, 0)
        pair(n_lat // 2 - 1, True)
    if rowsum_on_mxu:
        outs = [acc_ref[c][:, :vw] / acc_ref[c][:, vw:] for c in range(n_comp)]
    else:
        outs = [acc_ref[c] / jnp.sum(l_ref[c], axis=1, keepdims=True) for c in range(n_comp)]
    o_ref[...] = finalize(outs, *extras).astype(o_ref.dtype)


def _mla_finalize(outs):
    return outs[0]


def _diff_finalize(outs, lam_ref, g_ref, *, lam_init):
    lv = lam_ref[...]
    lam = (jnp.exp(jnp.sum(lv[0:1, :] * lv[1:2, :], axis=1, keepdims=True))
           - jnp.exp(jnp.sum(lv[2:3, :] * lv[3:4, :], axis=1, keepdims=True)) + lam_init)
    o = outs[0] - lam * outs[1]
    return o * lax.rsqrt(jnp.mean(o * o, axis=-1, keepdims=True) + EPS) * g_ref[...] * (1.0 - lam_init)


def _attention(finalize, n_comp, q, k, v, extra, heads, qw, vw, v_col0, batch, seq, ctx_len, latent, tq, tk):
    lat_ctx_blocks = batch * seq // ctx_len
    ctx_spec = lambda width, c0: pl.BlockSpec((ctx_len, width), lambda b, h, i: (lat_ctx_blocks + b, c0 + h))
    if latent:
        q_spec = pl.BlockSpec((tq, qw), lambda b, h, i: (b * (seq // tq) + i, h))
        in_specs = [q_spec, ctx_spec(qw, 0), ctx_spec(vw, v_col0),
                    pl.BlockSpec((seq, qw), lambda b, h, i: (b, h)),
                    pl.BlockSpec((seq, vw), lambda b, h, i: (b, v_col0 + h))]
        args = [q, k, v, k, v]
        grid = (batch, heads, seq // tq)
        out_spec = pl.BlockSpec((tq, vw), lambda b, h, i: (b * (seq // tq) + i, h))
        out_rows = batch * seq
        n_lat = seq // tk
        assert n_lat >= 2 and n_lat % 2 == 0
    else:
        tq = tk = ctx_len
        in_specs = [ctx_spec(qw, 0), ctx_spec(qw, 0), ctx_spec(vw, v_col0)]
        args = [q, k, v]
        grid = (batch, heads, 1)
        out_spec = pl.BlockSpec((ctx_len, vw), lambda b, h, i: (b, h))
        out_rows = batch * ctx_len
        n_lat = 0
    for e in extra:
        in_specs.append(pl.BlockSpec(e.shape, lambda b, h, i: (0, 0)))
        args.append(e)
    rowsum_on_mxu = vw == LANES
    scratch = [pltpu.VMEM((2, n_comp, tq, tk), F32),
               pltpu.VMEM((n_comp, tq, LANES), F32)]
    if rowsum_on_mxu:
        scratch.append(pltpu.VMEM((n_comp, tq, vw + LANES), F32))
    else:
        scratch += [pltpu.VMEM((n_comp, tq, LANES), F32),
                    pltpu.VMEM((n_comp, tq, vw), F32)]
    return pl.pallas_call(
        functools.partial(_attn_kernel, n_comp=n_comp, n_lat=n_lat, tk=tk, rowsum_on_mxu=rowsum_on_mxu,
                          finalize=finalize),
        grid=grid,
        in_specs=in_specs,
        out_specs=out_spec,
        out_shape=jax.ShapeDtypeStruct((out_rows, heads * vw), BF16),
        scratch_shapes=scratch,
        compiler_params=_cparams(("parallel", "parallel", "arbitrary")),
        name="attn_latent" if latent else "attn_ctx",
    )(*args)


def _rope_tables(rows, rot_dim):
    axis_dim = rot_dim // 2
    inv = ROPE_THETA ** (-jnp.arange(0, axis_dim, 2, dtype=F32) / axis_dim)
    row = jnp.repeat(jnp.arange(rows, dtype=F32), GRID_W)
    colv = (jnp.arange(rows * GRID_W) % GRID_W).astype(F32)
    ar, ac = row[:, None] * inv, colv[:, None] * inv
    cos = jnp.concatenate([jnp.cos(ar), jnp.cos(ar), jnp.cos(ac), jnp.cos(ac)], axis=1)
    sin = jnp.concatenate([-jnp.sin(ar), jnp.sin(ar), -jnp.sin(ac), jnp.sin(ac)], axis=1)
    return cos, sin


def _token_tables(cos, sin, batch, ctx_rows, left, right):
    n = cos.shape[0]
    cos = jnp.concatenate([jnp.ones((n, left), F32), cos, jnp.ones((n, right), F32)], axis=1)
    sin = jnp.concatenate([jnp.zeros((n, left), F32), sin, jnp.zeros((n, right), F32)], axis=1)
    w = cos.shape[1]
    cos = jnp.concatenate([jnp.tile(cos, (batch, 1)), jnp.ones((ctx_rows, w), F32)], axis=0)
    sin = jnp.concatenate([jnp.tile(sin, (batch, 1)), jnp.zeros((ctx_rows, w), F32)], axis=0)
    return cos, sin


def _layout_w_in(w):
    d = w.shape[0]
    w = w.astype(BF16)
    s1, s2 = SSD_COLS, SSD_COLS + MLA_COLS
    xbc0 = SSD_INNER
    z = w[:, :SSD_INNER]
    xs = w[:, xbc0:xbc0 + SSD_INNER]
    bm = w[:, xbc0 + SSD_INNER:xbc0 + SSD_INNER + SSD_GN]
    cm = w[:, xbc0 + SSD_INNER + SSD_GN:xbc0 + SSD_CONV_CH]
    dt = w[:, xbc0 + SSD_CONV_CH:s1]
    cq = w[:, s1:s1 + MLA_Q_LORA]
    ckv = w[:, s1 + MLA_Q_LORA:s1 + MLA_Q_LORA + MLA_KV_LORA]
    kr = w[:, s1 + MLA_Q_LORA + MLA_KV_LORA:s2]
    qk = 2 * DIFF_HEADS * DIFF_QK
    dq, dk, dv = w[:, s2:s2 + qk], w[:, s2 + qk:s2 + 2 * qk], w[:, s2 + 2 * qk:]
    zeros = lambda n: jnp.zeros((d, n), w.dtype)
    main = jnp.concatenate([z, xs, dq, dk, dv, cq, bm, cm, ckv, kr, zeros(LANES - MLA_ROPE), zeros(LANES)], axis=1)
    dtw = jnp.concatenate([dt, zeros(LANES - 2 * SSD_HEADS)], axis=1)
    return main, dtw


def _layout_mla(w_uq, w_ukv, g_q, g_k):
    pad = MLA_QK_PAD - MLA_QK
    uq = jnp.pad(w_uq.reshape(MLA_Q_LORA, MLA_HEADS, MLA_QK), ((0, 0), (0, 0), (0, pad)))
    ukv = w_ukv.reshape(MLA_KV_LORA, MLA_HEADS, MLA_NOPE + MLA_V)
    return dict(
        w_uq=uq.reshape(MLA_Q_LORA, MLA_HEADS * MLA_QK_PAD).astype(BF16),
        w_ukn=ukv[:, :, :MLA_NOPE].reshape(MLA_KV_LORA, MLA_HEADS * MLA_NOPE).astype(BF16),
        w_uv=ukv[:, :, MLA_NOPE:].reshape(MLA_KV_LORA, MLA_HEADS * MLA_V).astype(BF16),
        g_q=jnp.pad(g_q, (0, pad)).reshape(1, MLA_QK_PAD),
        g_k=jnp.pad(g_k, (0, pad)).reshape(1, MLA_QK_PAD),
    )


def kernel(x, c, ctx, c_ctx, w_mod, b_mod, g_norm, w_ffn_gate, w_ffn_up, w_ffn_down, w_in, ssd_conv_w, ssd_conv_b, ssd_dt_bias, ssd_a_log, ssd_d, ssd_norm_g, mla_g_cq, mla_w_uq, mla_g_ckv, mla_w_ukv, mla_g_q, mla_g_k, diff_g_q, diff_g_k, diff_lambda, diff_subln_g, w_out):
    batch, seq, d = x.shape
    ctx_len = ctx.shape[1]
    depth = w_mod.shape[0]
    d_ff = w_ffn_gate.shape[-1]
    lat_rows, ctx_rows = batch * seq, batch * ctx_len
    n_rows = lat_rows + ctx_rows
    assert w_in.shape[-1] == IN_COLS and w_out.shape[1] == SSD_INNER + MLA_HEADS * MLA_V + DIFF_HEADS * DIFF_V
    assert seq % GRID_W == 0 and batch < MOD_ROWS
    tm = _pick(math.gcd(seq, ctx_rows), (512, 256, 128))
    tr = _pick(math.gcd(seq, ctx_len), (256, 128))
    assert seq % tm == 0 and ctx_rows % tm == 0 and seq % tr == 0 and ctx_len % tr == 0
    assert ctx_len % SSD_CHUNK == 0 and seq % SSD_CHUNK == 0 and lat_rows % ctx_len == 0
    tq = _pick(seq, (512, 256, 128))
    tk = _pick(seq // 2, (2048, 1024, 512, 256, 128))

    def group_of_row(r):
        return jnp.minimum(r // seq, batch)

    xs = (x.reshape(lat_rows, d), ctx.reshape(ctx_rows, d))
    c_rows = jnp.concatenate([c, c_ctx[None, :], jnp.zeros((MOD_ROWS - batch - 1, d), F32)], axis=0)
    mod = _modulation(c_rows, w_mod, b_mod).reshape(depth, MOD_ROWS, 1, N_MOD * d)

    rows = seq // GRID_W
    cos_m, sin_m = _token_tables(*_rope_tables(rows, MLA_ROPE), batch, ctx_rows, 0, LANES - MLA_ROPE)
    cos_d, sin_d = _token_tables(*_rope_tables(rows, DIFF_QK), batch, ctx_rows, 0, 0)

    tn_ff = _pick(d_ff, (1024, 512, 256, 128))
    tn_out = _pick(d, (1024, 512, 256, 128))
    tn_in = _pick(PROJ_COLS, (768,))

    w_gate, w_up, w_down = w_ffn_gate.astype(BF16), w_ffn_up.astype(BF16), w_ffn_down.astype(BF16)

    def ffn(xs, li, j, norm_idx, gate_idx, m_rows):
        h = _norm_mod(xs, g_norm[li, norm_idx], mod[li], norm_idx, m_rows, group_of_row, tr)
        act = _matmul_swiglu(h, w_gate, w_up, (li, j), m_rows, tm, tn_ff)
        return _matmul_resid([(act, w_down, (li, j))], xs, mod[li], gate_idx, 0.5, m_rows, group_of_row, tm, tn_out,
                             single_buffer=True)

    for li in range(depth):
        need_ctx = li < depth - 1
        out_rows = n_rows if need_ctx else lat_rows
        lam_init = 0.8 - 0.6 * math.exp(-0.3 * li)

        xs = (ffn(xs, li, 0, 0, 2, n_rows),)

        h = _norm_mod(xs, g_norm[li, 1], mod[li], 1, n_rows, group_of_row, tr)
        w_main, w_dt = _layout_w_in(w_in[li])
        proj = _matmul(h, w_main, n_rows, BF16, tm, tn_in)
        dt_raw = _matmul(h, w_dt, n_rows, F32, tm, LANES)

        xbc = _ssd_conv(proj, ssd_conv_w[li], ssd_conv_b[li], n_rows, lat_rows, seq, ctx_len, tr)
        yf, yb = _ssd_scan(xbc, dt_raw, ssd_dt_bias[li], ssd_a_log[li], batch, seq, ctx_len)
        ssd_o = _ssd_out(yf, yb, xbc, proj, ssd_d[li], ssd_norm_g[li], out_rows, tr)

        mla_p = _layout_mla(mla_w_uq[li], mla_w_ukv[li], mla_g_q[li], mla_g_k[li])
        mla_p['g_cq'] = mla_g_cq[li].reshape(1, MLA_Q_LORA)
        mla_p['g_ckv'] = mla_g_ckv[li].reshape(1, MLA_KV_LORA)
        mq, mk, mv = _mla_prep(proj, mla_p, cos_m, sin_m, n_rows, tr)
        mla_args = (_mla_finalize, 1, mq, mk, mv, [], MLA_HEADS, MLA_QK_PAD, MLA_V, 0, batch, seq, ctx_len)
        mla_o = _attention(*mla_args, True, tq, tk)

        dq, dk = _diff_prep(proj, diff_g_q[li], diff_g_k[li], cos_d, sin_d, n_rows, tr)
        diff_fin = functools.partial(_diff_finalize, lam_init=lam_init)
        extra = [diff_lambda[li], diff_subln_g[li].reshape(1, DIFF_V)]
        diff_args = (diff_fin, 2, dq, dk, proj, extra, DIFF_HEADS, 2 * DIFF_QK, DIFF_V, PC_DV // DIFF_V,
                     batch, seq, ctx_len)
        diff_o = _attention(*diff_args, True, tq, tk)
        if need_ctx:
            mla_o = jnp.concatenate([mla_o, _attention(*mla_args, False, tq, tk)], axis=0)
            diff_o = jnp.concatenate([diff_o, _attention(*diff_args, False, tq, tk)], axis=0)

        wo = w_out[li].astype(BF16)
        o1, o2 = SSD_INNER, SSD_INNER + MLA_HEADS * MLA_V
        xs = _matmul_resid([(ssd_o, wo[:o1], ()), (mla_o, wo[o1:o2], ()), (diff_o, wo[o2:], ())], xs, mod[li], 5,
                           1.0, out_rows, group_of_row, tm, tn_out)

        xs = (ffn((xs,), li, 1, 2, 8, out_rows),)

    return xs[0][:lat_rows].reshape(batch, seq, d)
```

```python
import functools
import math

import jax
import jax.numpy as jnp
from jax import lax
from jax.experimental import pallas as pl
from jax.experimental.pallas import tpu as pltpu

F32 = jnp.float32
BF16 = jnp.bfloat16

GRID_W = 64
EPS = 1e-6
ROPE_THETA = 10000.0
N_MOD = 9
SSD_INNER = 2048
SSD_HEAD_DIM = 64
SSD_HEADS = 32
SSD_GROUPS = 4
SSD_STATE = 128
SSD_CONV = 5
SSD_CHUNK = 128
SSD_GN = SSD_GROUPS * SSD_STATE
SSD_CONV_CH = SSD_INNER + 2 * SSD_GN
SSD_COLS = 2 * SSD_INNER + 2 * SSD_GN + 2 * SSD_HEADS
MLA_HEADS = 8
MLA_NOPE = 128
MLA_ROPE = 64
MLA_V = 128
MLA_Q_LORA = 1024
MLA_KV_LORA = 512
MLA_QK = MLA_NOPE + MLA_ROPE
MLA_QK_PAD = 256
MLA_SCALE = 1.0 / math.sqrt(MLA_QK)
LOG2E = math.log2(math.e)
MLA_COLS = MLA_Q_LORA + MLA_KV_LORA + MLA_ROPE
DIFF_HEADS = 4
DIFF_QK = 128
DIFF_V = 2 * DIFF_QK
DIFF_SCALE = 1.0 / math.sqrt(DIFF_QK)
DIFF_COLS = DIFF_HEADS * (4 * DIFF_QK + DIFF_V)
IN_COLS = SSD_COLS + MLA_COLS + DIFF_COLS

LANES = 128
BF16_SUBLANES = 16
MOD_ROWS = 8
VMEM_LIMIT = 56 * 1024 * 1024

PC_Z = 0
PC_XS = 2048
PC_B = 4096
PC_C = 4608
PC_DQ = 5120
PC_DK = 6144
PC_DV = 7168
PC_CQ = 8192
PC_CKV = 9216
PC_KR = 9728
PC_PAD = 9856
PROJ_COLS = 9984


def _cparams(sem):
    return pltpu.CompilerParams(dimension_semantics=sem, vmem_limit_bytes=VMEM_LIMIT)


def _pick(n, prefs):
    for p in prefs:
        if n % p == 0:
            return p
    return n


def _silu(x):
    return x / (1.0 + jnp.exp(-x))


def _weight_spec(w, lead, tn, single_buffer=False):
    mode = dict(pipeline_mode=pl.Buffered(1)) if single_buffer else {}
    return pl.BlockSpec((None,) * len(lead) + (w.shape[-2], tn), lambda j, i: tuple(lead) + (0, j), **mode)


def _mm_kernel(a_ref, b_ref, o_ref):
    o_ref[...] = jnp.dot(a_ref[...], b_ref[...], preferred_element_type=F32).astype(o_ref.dtype)


def _matmul(a, b, m_rows, out_dtype, tm, tn):
    k = a.shape[1]
    n = b.shape[1]
    return pl.pallas_call(
        _mm_kernel,
        grid=(n // tn, m_rows // tm),
        in_specs=[pl.BlockSpec((tm, k), lambda j, i: (i, 0)),
                  pl.BlockSpec((k, tn), lambda j, i: (0, j))],
        out_specs=pl.BlockSpec((tm, tn), lambda j, i: (i, j)),
        out_shape=jax.ShapeDtypeStruct((m_rows, n), out_dtype),
        compiler_params=_cparams(("parallel", "parallel")),
        name="mm_plain",
    )(a, b)


def _swiglu_kernel(a_ref, wg_ref, wu_ref, o_ref):
    a = a_ref[...]
    g = jnp.dot(a, wg_ref[...], preferred_element_type=F32)
    u = jnp.dot(a, wu_ref[...], preferred_element_type=F32)
    o_ref[...] = (_silu(g) * u).astype(o_ref.dtype)


def _matmul_swiglu(a, wg, wu, lead, m_rows, tm, tn):
    k = a.shape[1]
    n = wg.shape[-1]
    return pl.pallas_call(
        _swiglu_kernel,
        grid=(n // tn, m_rows // tm),
        in_specs=[pl.BlockSpec((tm, k), lambda j, i: (i, 0)),
                  _weight_spec(wg, lead, tn), _weight_spec(wu, lead, tn)],
        out_specs=pl.BlockSpec((tm, tn), lambda j, i: (i, j)),
        out_shape=jax.ShapeDtypeStruct((m_rows, n), BF16),
        compiler_params=_cparams(("parallel", "parallel")),
        name="mm_swiglu",
    )(a, wg, wu)


def _row_source_specs(parts, t, width, col):
    if len(parts) == 1:
        return [pl.BlockSpec((t, width), lambda *ids: (ids[-1], col(*ids)))]
    n0 = parts[0].shape[0] // t
    return [pl.BlockSpec((t, width), lambda *ids: (jnp.minimum(ids[-1], n0 - 1), col(*ids))),
            pl.BlockSpec((t, width), lambda *ids: (jnp.maximum(ids[-1] - n0, 0), col(*ids)))]


def _resid_kernel(*refs, n_pairs, n_src, n_first, scale):
    x_refs = refs[2 * n_pairs:2 * n_pairs + n_src]
    gate_ref, o_ref = refs[2 * n_pairs + n_src], refs[2 * n_pairs + n_src + 1]
    acc = jnp.dot(refs[0][...], refs[1][...], preferred_element_type=F32)
    for p in range(1, n_pairs):
        acc = acc + jnp.dot(refs[2 * p][...], refs[2 * p + 1][...], preferred_element_type=F32)
    x = x_refs[0][...]
    if n_src == 2:
        x = jnp.where(pl.program_id(1) < n_first, x, x_refs[1][...])
    o_ref[...] = x + (scale * gate_ref[...]) * acc


def _matmul_resid(pairs, x, mod4, gate_idx, scale, m_rows, group_of_row, tm, tn, single_buffer=False):
    d = x[0].shape[1]
    in_specs = []
    args = []
    for a, b, lead in pairs:
        in_specs.append(pl.BlockSpec((tm, a.shape[1]), lambda j, i: (i, 0)))
        in_specs.append(_weight_spec(b, lead, tn, single_buffer))
        args += [a, b]
    in_specs += _row_source_specs(x, tm, tn, lambda j, i: j)
    gate_blk = gate_idx * (d // tn)
    in_specs.append(pl.BlockSpec((None, 1, tn), lambda j, i: (group_of_row(i * tm), 0, gate_blk + j)))
    args += [*x, mod4]
    return pl.pallas_call(
        functools.partial(_resid_kernel, n_pairs=len(pairs), n_src=len(x), n_first=x[0].shape[0] // tm,
                          scale=scale),
        grid=(d // tn, m_rows // tm),
        in_specs=in_specs,
        out_specs=pl.BlockSpec((tm, tn), lambda j, i: (i, j)),
        out_shape=jax.ShapeDtypeStruct((m_rows, d), F32),
        compiler_params=_cparams(("parallel", "parallel")),
        name="mm_resid",
    )(*args)


def _mod_kernel(c_ref, w_ref, b_ref, o_ref):
    s = _silu(c_ref[...]).astype(BF16)
    w = w_ref[...].astype(BF16)
    o_ref[...] = jnp.dot(s, w, preferred_element_type=F32) + b_ref[...]


def _modulation(c_rows, w_mod, b_mod):
    depth, d, n = w_mod.shape
    tn = _pick(n, (512, 256, 128))
    return pl.pallas_call(
        _mod_kernel,
        grid=(depth, n // tn),
        in_specs=[pl.BlockSpec((MOD_ROWS, d), lambda l, j: (0, 0)),
                  pl.BlockSpec((None, d, tn), lambda l, j: (l, 0, j)),
                  pl.BlockSpec((None, 1, tn), lambda l, j: (l, 0, j))],
        out_specs=pl.BlockSpec((None, MOD_ROWS, tn), lambda l, j: (l, 0, j)),
        out_shape=jax.ShapeDtypeStruct((depth, MOD_ROWS, n), F32),
        compiler_params=_cparams(("parallel", "parallel")),
        name="modulation",
    )(c_rows, w_mod, b_mod.reshape(depth, 1, n))


def _norm_mod_kernel(*refs, n_first):
    x_refs = refs[:-4]
    g_ref, shift_ref, scale_ref, o_ref = refs[-4:]
    first = pl.program_id(0) < n_first
    gain = g_ref[...] * (1.0 + scale_ref[...])
    shift = shift_ref[...]
    rows = BF16_SUBLANES

    def body(i, carry):
        r = pl.multiple_of(i * rows, rows)
        x = x_refs[0][pl.ds(r, rows), :]
        if len(x_refs) == 2:
            x = jnp.where(first, x, x_refs[1][pl.ds(r, rows), :])
        y = x * lax.rsqrt(jnp.mean(x * x, axis=-1, keepdims=True) + EPS)
        o_ref[pl.ds(r, rows), :] = (y * gain + shift).astype(o_ref.dtype)
        return carry

    lax.fori_loop(0, o_ref.shape[0] // rows, body, 0, unroll=2)


def _norm_mod(x, g, mod4, idx, m_rows, group_of_row, tm):
    d = x[0].shape[1]
    return pl.pallas_call(
        functools.partial(_norm_mod_kernel, n_first=x[0].shape[0] // tm),
        grid=(m_rows // tm,),
        in_specs=_row_source_specs(x, tm, d, lambda i: 0) + [
                  pl.BlockSpec((1, d), lambda i: (0, 0)),
                  pl.BlockSpec((None, 1, d), lambda i: (group_of_row(i * tm), 0, 3 * idx)),
                  pl.BlockSpec((None, 1, d), lambda i: (group_of_row(i * tm), 0, 3 * idx + 1))],
        out_specs=pl.BlockSpec((tm, d), lambda i: (i, 0)),
        out_shape=jax.ShapeDtypeStruct((m_rows, d), BF16),
        compiler_params=_cparams(("parallel",)),
        name="norm_mod",
    )(*x, g.reshape(1, d), mod4, mod4)


def _conv_kernel(prev_ref, cur_ref, next_ref, w_ref, b_ref, o_ref, ext_ref, *, tr, lat_rows, seq, ctx_len):
    r0 = pl.program_id(0) * tr
    in_lat = r0 < lat_rows
    off = jnp.where(in_lat, r0 % seq, (r0 - lat_rows) % ctx_len)
    seg = jnp.where(in_lat, seq, ctx_len)
    first = off == 0
    last = off + tr == seg
    h = BF16_SUBLANES
    ext_ref[0:h, :] = jnp.where(first, 0.0, prev_ref[...].astype(F32))
    ext_ref[h:h + tr, :] = cur_ref[...].astype(F32)
    ext_ref[h + tr:2 * h + tr, :] = jnp.where(last, 0.0, next_ref[...].astype(F32))
    acc = b_ref[...] + w_ref[0:1, :] * ext_ref[pl.ds(h - 2, tr), :]
    for k in range(1, SSD_CONV):
        acc = acc + w_ref[k:k + 1, :] * ext_ref[pl.ds(h - 2 + k, tr), :]
    o_ref[...] = _silu(acc).astype(o_ref.dtype)


def _ssd_conv(proj, conv_w, conv_b, n_rows, lat_rows, seq, ctx_len, tr):
    tc = 1024
    h = BF16_SUBLANES
    rb = tr // h
    n_halo = n_rows // h
    xs_blk, bc_blk = PC_XS // tc, PC_B // tc
    n_xs = SSD_INNER // tc

    def col(j):
        return jnp.where(j < n_xs, xs_blk + j, bc_blk + (j - n_xs))

    return pl.pallas_call(
        functools.partial(_conv_kernel, tr=tr, lat_rows=lat_rows, seq=seq, ctx_len=ctx_len),
        grid=(n_rows // tr, SSD_CONV_CH // tc),
        in_specs=[pl.BlockSpec((h, tc), lambda i, j: (jnp.maximum(i * rb - 1, 0), col(j))),
                  pl.BlockSpec((tr, tc), lambda i, j: (i, col(j))),
                  pl.BlockSpec((h, tc), lambda i, j: (jnp.minimum((i + 1) * rb, n_halo - 1), col(j))),
                  pl.BlockSpec((SSD_CONV, tc), lambda i, j: (0, j)),
                  pl.BlockSpec((1, tc), lambda i, j: (0, j))],
        out_specs=pl.BlockSpec((tr, tc), lambda i, j: (i, j)),
        out_shape=jax.ShapeDtypeStruct((n_rows, SSD_CONV_CH), BF16),
        scratch_shapes=[pltpu.VMEM((tr + 2 * h, tc), F32)],
        compiler_params=_cparams(("parallel", "parallel")),
        name="ssd_conv",
    )(proj, proj, proj, conv_w, conv_b.reshape(1, SSD_CONV_CH))


N_PAIRS = SSD_HEADS // 2
PAIRS_PER_GROUP = N_PAIRS // SSD_GROUPS


def _ssd_direction(d, x_ref, b_ref, c_ref, dt_ref, bias_ref, alog_ref, y_ref, st_ref,
                   cs_ref, cst_ref, dtt_ref, woutt_ref, g_ref, bt_ref, yoff_ref, xp_ref, yp_ref):
    q = SSD_CHUNK
    lane = lax.broadcasted_iota(jnp.int32, (q, LANES), 1)
    row = lax.broadcasted_iota(jnp.int32, (q, q), 0)
    col = lax.broadcasted_iota(jnp.int32, (q, q), 1)
    mask = (row >= col) if d == 0 else (row <= col)

    z = dt_ref[...] + bias_ref[...]
    dt = jnp.maximum(z, 0.0) + jnp.log1p(jnp.exp(-jnp.abs(z)))
    a_dt = dt * (-jnp.exp(alog_ref[...]))
    hi = a_dt.astype(BF16)
    r1 = a_dt - hi.astype(F32)
    mid = r1.astype(BF16)
    lo = (r1 - mid.astype(F32)).astype(BF16)
    tri = jnp.where(mask, 1.0, 0.0).astype(BF16)
    c3 = jnp.dot(tri, jnp.concatenate([hi, mid, lo], axis=1), preferred_element_type=F32)
    cs = c3[:, 0:LANES] + c3[:, LANES:2 * LANES] + c3[:, 2 * LANES:3 * LANES]
    end = q - 1 if d == 0 else 0
    a_tot = cs[end:end + 1, :]
    cs_ref[...] = cs
    cst_ref[...] = cs.T
    dtt_ref[...] = dt.T
    woutt_ref[...] = (dt * jnp.exp(a_tot - cs)).T

    for g in range(SSD_GROUPS):
        bg = b_ref[:, g * SSD_STATE:(g + 1) * SSD_STATE]
        cg = c_ref[:, g * SSD_STATE:(g + 1) * SSD_STATE]
        g_ref[g] = lax.dot_general(cg, bg, (((1,), (1,)), ((), ())), preferred_element_type=F32)
        bt_ref[g] = bg.astype(F32).T
        for pp in range(PAIRS_PER_GROUP):
            p = g * PAIRS_PER_GROUP + pp
            yoff_ref[p] = jnp.dot(cg, st_ref[p].astype(BF16), preferred_element_type=F32)
    for p in range(N_PAIRS):
        xp_ref[p] = x_ref[:, p * LANES:(p + 1) * LANES]

    half = lane < SSD_HEAD_DIM

    def pair_body(p, carry):
        g = p // PAIRS_PER_GROUP
        h0 = 2 * p + SSD_HEADS * d
        gmat = g_ref[g]
        bt = bt_ref[g]
        xp = xp_ref[p]
        ys, ups, dins = [], [], []
        for hh in (h0, h0 + 1):
            cs_col = jnp.sum(jnp.where(lane == hh, cs_ref[...], 0.0), axis=1, keepdims=True)
            lmat = jnp.exp(jnp.where(mask, cs_col - cst_ref[pl.ds(hh, 1), :], -1e30))
            m = (gmat * lmat * dtt_ref[pl.ds(hh, 1), :]).astype(BF16)
            ys.append(jnp.dot(m, xp, preferred_element_type=F32))
            bw = (bt * woutt_ref[pl.ds(hh, 1), :]).astype(BF16)
            ups.append(jnp.dot(bw, xp, preferred_element_type=F32))
            dins.append(jnp.exp(cs_col))
        din = jnp.where(half, dins[0], dins[1])
        yp_ref[p] = jnp.where(half, ys[0], ys[1]) + yoff_ref[p] * din
        st_ref[p] = st_ref[p] * din[end:end + 1, :] + jnp.where(half, ups[0], ups[1])
        return carry

    lax.fori_loop(0, N_PAIRS, pair_body, 0, unroll=4)
    for p in range(N_PAIRS):
        y_ref[:, p * LANES:(p + 1) * LANES] = yp_ref[p]


def _ssd_kernel(xf_ref, bf_ref, cf_ref, dtf_ref, xb_ref, bb_ref, cb_ref, dtb_ref, bias_ref, alog_ref,
                yf_ref, yb_ref, stf_ref, stb_ref, cs_ref, cst_ref, dtt_ref, woutt_ref,
                g_ref, bt_ref, yoff_ref, xp_ref, yp_ref):
    @pl.when(pl.program_id(1) == 0)
    def _():
        stf_ref[...] = jnp.zeros_like(stf_ref)
        stb_ref[...] = jnp.zeros_like(stb_ref)

    shared = (cs_ref, cst_ref, dtt_ref, woutt_ref, g_ref, bt_ref, yoff_ref, xp_ref, yp_ref)
    _ssd_direction(0, xf_ref, bf_ref, cf_ref, dtf_ref, bias_ref, alog_ref, yf_ref, stf_ref, *shared)
    _ssd_direction(1, xb_ref, bb_ref, cb_ref, dtb_ref, bias_ref, alog_ref, yb_ref, stb_ref, *shared)


def _ssd_scan(xbc, dt_raw, dt_bias, a_log, batch, seq, ctx_len):
    q = SSD_CHUNK
    n_rows = xbc.shape[0]
    ncc, nlc = ctx_len // q, seq // q
    lat_blocks = batch * seq // q

    def fwd_blk(b, s):
        return jnp.where(s < ncc, lat_blocks + b * ncc + s, b * nlc + (s - ncc))

    def bwd_blk(b, s):
        return jnp.where(s < ncc, lat_blocks + b * ncc + (ncc - 1 - s), b * nlc + (nlc - 1 - (s - ncc)))

    def specs(blk):
        return [pl.BlockSpec((q, SSD_INNER), lambda b, s: (blk(b, s), 0)),
                pl.BlockSpec((q, SSD_GN), lambda b, s: (blk(b, s), SSD_INNER // SSD_GN)),
                pl.BlockSpec((q, SSD_GN), lambda b, s: (blk(b, s), SSD_INNER // SSD_GN + 1)),
                pl.BlockSpec((q, LANES), lambda b, s: (blk(b, s), 0))]

    pad = jnp.zeros((LANES - 2 * SSD_HEADS,), F32)
    bias = jnp.concatenate([dt_bias.reshape(-1), pad]).reshape(1, LANES)
    alog = jnp.concatenate([a_log.reshape(-1), pad]).reshape(1, LANES)
    tile = (q, LANES)
    return pl.pallas_call(
        _ssd_kernel,
        grid=(batch, ncc + nlc),
        in_specs=specs(fwd_blk) + specs(bwd_blk) + [pl.BlockSpec((1, LANES), lambda b, s: (0, 0))] * 2,
        out_specs=[pl.BlockSpec((q, SSD_INNER), lambda b, s: (fwd_blk(b, s), 0)),
                   pl.BlockSpec((q, SSD_INNER), lambda b, s: (bwd_blk(b, s), 0))],
        out_shape=[jax.ShapeDtypeStruct((n_rows, SSD_INNER), F32)] * 2,
        scratch_shapes=[pltpu.VMEM((N_PAIRS, SSD_STATE, LANES), F32),
                        pltpu.VMEM((N_PAIRS, SSD_STATE, LANES), F32),
                        pltpu.VMEM(tile, F32), pltpu.VMEM(tile, F32), pltpu.VMEM(tile, F32),
                        pltpu.VMEM(tile, F32),
                        pltpu.VMEM((SSD_GROUPS, q, q), F32),
                        pltpu.VMEM((SSD_GROUPS, SSD_STATE, q), F32),
                        pltpu.VMEM((N_PAIRS, q, LANES), F32),
                        pltpu.VMEM((N_PAIRS, q, LANES), BF16),
                        pltpu.VMEM((N_PAIRS, q, LANES), F32)],
        compiler_params=_cparams(("parallel", "arbitrary")),
        name="ssd_scan",
    )(xbc, xbc, xbc, dt_raw, xbc, xbc, xbc, dt_raw, bias, alog)


def _ssd_out_kernel(yf_ref, yb_ref, x_ref, z_ref, d_ref, g_ref, o_ref):
    z = z_ref[...].astype(F32)
    y = (yf_ref[...] + yb_ref[...] + d_ref[...] * x_ref[...].astype(F32)) * _silu(z)
    y = y * lax.rsqrt(jnp.mean(y * y, axis=-1, keepdims=True) + EPS) * g_ref[...]
    o_ref[...] = y.astype(o_ref.dtype)


def _ssd_out(yf, yb, xbc, proj, ssd_d, norm_g, n_rows, tm):
    w = SSD_INNER
    row = lambda i: (i, 0)
    return pl.pallas_call(
        _ssd_out_kernel,
        grid=(n_rows // tm,),
        in_specs=[pl.BlockSpec((tm, w), row), pl.BlockSpec((tm, w), row), pl.BlockSpec((tm, w), row),
                  pl.BlockSpec((tm, w), lambda i: (i, PC_Z // w)),
                  pl.BlockSpec((1, w), lambda i: (0, 0)), pl.BlockSpec((1, w), lambda i: (0, 0))],
        out_specs=pl.BlockSpec((tm, w), row),
        out_shape=jax.ShapeDtypeStruct((n_rows, w), BF16),
        compiler_params=_cparams(("parallel",)),
        name="ssd_out",
    )(yf, yb, xbc, proj, jnp.repeat(ssd_d, SSD_HEAD_DIM).reshape(1, w), norm_g.reshape(1, w))


def _rms(x, g, n):
    return x * lax.rsqrt(jnp.sum(x * x, axis=-1, keepdims=True) * (1.0 / n) + EPS) * g


def _rope(x, cos, sin_signed, half):
    w = x.shape[1]
    lane = lax.broadcasted_iota(jnp.int32, x.shape, 1)
    partner = jnp.where(lane % (2 * half) < half, pltpu.roll(x, w - half, 1), pltpu.roll(x, half, 1))
    return x * cos + partner * sin_signed


def _mla_prep_kernel(cq_ref, ckv_ref, kr_ref, gcq_ref, wuq_ref, gckv_ref, wukn_ref, wuv_ref, gq_ref, gk_ref,
                     cos_ref, sin_ref, q_ref, k_ref, v_ref):
    hq = _rms(cq_ref[...].astype(F32), gcq_ref[...], MLA_Q_LORA).astype(BF16)
    q_all = jnp.dot(hq, wuq_ref[...], preferred_element_type=F32)
    hkv = _rms(ckv_ref[...].astype(F32), gckv_ref[...], MLA_KV_LORA).astype(BF16)
    kn_all = jnp.dot(hkv, wukn_ref[...], preferred_element_type=F32)
    v_ref[...] = jnp.dot(hkv, wuv_ref[...], preferred_element_type=F32).astype(v_ref.dtype)
    kr = kr_ref[...].astype(F32)
    lane = lax.broadcasted_iota(jnp.int32, kr.shape, 1)
    kr = jnp.where(lane < MLA_ROPE, kr, 0.0)
    cos, sin = cos_ref[...], sin_ref[...]
    w = MLA_QK_PAD
    n = MLA_NOPE

    def norm_rope(lo, hi, g):
        ss = jnp.sum(lo * lo, axis=-1, keepdims=True) + jnp.sum(hi * hi, axis=-1, keepdims=True)
        r = lax.rsqrt(ss * (1.0 / MLA_QK) + EPS)
        return lo * r * g[:, :n], _rope(hi * r * g[:, n:], cos, sin, MLA_ROPE // 4)

    for h in range(MLA_HEADS):
        lo, hi = norm_rope(q_all[:, h * w:h * w + n], q_all[:, h * w + n:(h + 1) * w], gq_ref[...])
        q_ref[:, h * w:h * w + n] = (lo * (MLA_SCALE * LOG2E)).astype(q_ref.dtype)
        q_ref[:, h * w + n:(h + 1) * w] = (hi * (MLA_SCALE * LOG2E)).astype(q_ref.dtype)
        lo, hi = norm_rope(kn_all[:, h * n:(h + 1) * n], kr, gk_ref[...])
        k_ref[:, h * w:h * w + n] = lo.astype(k_ref.dtype)
        k_ref[:, h * w + n:(h + 1) * w] = hi.astype(k_ref.dtype)


def _mla_prep(proj, p, cos, sin, n_rows, tm):
    hw = MLA_HEADS * MLA_QK_PAD
    hv = MLA_HEADS * MLA_V
    full = lambda i: (0, 0)
    return pl.pallas_call(
        _mla_prep_kernel,
        grid=(n_rows // tm,),
        in_specs=[pl.BlockSpec((tm, MLA_Q_LORA), lambda i: (i, PC_CQ // MLA_Q_LORA)),
                  pl.BlockSpec((tm, MLA_KV_LORA), lambda i: (i, PC_CKV // MLA_KV_LORA)),
                  pl.BlockSpec((tm, LANES), lambda i: (i, PC_KR // LANES)),
                  pl.BlockSpec((1, MLA_Q_LORA), full), pl.BlockSpec((MLA_Q_LORA, hw), full),
                  pl.BlockSpec((1, MLA_KV_LORA), full), pl.BlockSpec((MLA_KV_LORA, hv), full),
                  pl.BlockSpec((MLA_KV_LORA, hv), full),
                  pl.BlockSpec((1, MLA_QK_PAD), full), pl.BlockSpec((1, MLA_QK_PAD), full),
                  pl.BlockSpec((tm, LANES), lambda i: (i, 0)),
                  pl.BlockSpec((tm, LANES), lambda i: (i, 0))],
        out_specs=[pl.BlockSpec((tm, hw), lambda i: (i, 0)), pl.BlockSpec((tm, hw), lambda i: (i, 0)),
                   pl.BlockSpec((tm, hv), lambda i: (i, 0))],
        out_shape=[jax.ShapeDtypeStruct((n_rows, hw), BF16), jax.ShapeDtypeStruct((n_rows, hw), BF16),
                   jax.ShapeDtypeStruct((n_rows, hv), BF16)],
        compiler_params=_cparams(("parallel",)),
        name="mla_prep",
    )(proj, proj, proj, p['g_cq'], p['w_uq'], p['g_ckv'], p['w_ukn'], p['w_uv'], p['g_q'], p['g_k'], cos, sin)


def _diff_prep_kernel(q_in, k_in, gq_ref, gk_ref, cos_ref, sin_ref, q_ref, k_ref):
    cos, sin = cos_ref[...], sin_ref[...]
    w = DIFF_QK
    for c in range(2 * DIFF_HEADS):
        sl = slice(c * w, (c + 1) * w)
        qc = _rope(_rms(q_in[:, sl].astype(F32), gq_ref[...], w), cos, sin, w // 4)
        q_ref[:, sl] = (qc * (DIFF_SCALE * LOG2E)).astype(q_ref.dtype)
        kc = _rope(_rms(k_in[:, sl].astype(F32), gk_ref[...], w), cos, sin, w // 4)
        k_ref[:, sl] = kc.astype(k_ref.dtype)


def _diff_prep(proj, g_q, g_k, cos, sin, n_rows, tm):
    w = 2 * DIFF_HEADS * DIFF_QK
    full = lambda i: (0, 0)
    row = lambda i: (i, 0)
    return pl.pallas_call(
        _diff_prep_kernel,
        grid=(n_rows // tm,),
        in_specs=[pl.BlockSpec((tm, w), lambda i: (i, PC_DQ // w)),
                  pl.BlockSpec((tm, w), lambda i: (i, PC_DK // w)),
                  pl.BlockSpec((1, DIFF_QK), full), pl.BlockSpec((1, DIFF_QK), full),
                  pl.BlockSpec((tm, DIFF_QK), row), pl.BlockSpec((tm, DIFF_QK), row)],
        out_specs=[pl.BlockSpec((tm, w), row), pl.BlockSpec((tm, w), row)],
        out_shape=[jax.ShapeDtypeStruct((n_rows, w), BF16)] * 2,
        compiler_params=_cparams(("parallel",)),
        name="diff_prep",
    )(proj, proj, g_q.reshape(1, DIFF_QK), g_k.reshape(1, DIFF_QK), cos, sin)


def _scores(q, k):
    return lax.dot_general(q, k, (((1,), (1,)), ((), ())), preferred_element_type=F32)


def _online_update(s, v, m_ref, l_ref, acc_ref, c):
    tiles = [s[:, t * LANES:(t + 1) * LANES] for t in range(s.shape[1] // LANES)]
    m_old = m_ref[c]
    m_new = jnp.maximum(m_old, jnp.max(functools.reduce(jnp.maximum, tiles), axis=1, keepdims=True))
    alpha = jnp.exp2(m_old - m_new)
    m_ref[c] = m_new
    ps = [jnp.exp2(t - m_new) for t in tiles]
    if l_ref is None:
        v = jnp.concatenate([v, jnp.ones((v.shape[0], LANES), v.dtype)], axis=1)
    else:
        l_ref[c] = alpha * l_ref[c] + functools.reduce(jnp.add, ps)
    pv = jnp.dot(jnp.concatenate(ps, axis=1).astype(BF16), v, preferred_element_type=F32)
    reps = acc_ref.shape[-1] // LANES
    acc_ref[c] = acc_ref[c] * jnp.concatenate([alpha] * reps, axis=1) + pv


def _attn_kernel(*refs, n_comp, n_lat, tk, rowsum_on_mxu, finalize):
    q_ref, kc_ref, vc_ref = refs[:3]
    n_in = 3
    if n_lat:
        kl_ref, vl_ref = refs[3:5]
        n_in = 5
    n_scratch = 3 if rowsum_on_mxu else 4
    extras = refs[n_in:-n_scratch - 1]
    o_ref, s_ref, m_ref, acc_ref = refs[-n_scratch - 1], refs[-n_scratch], refs[-n_scratch + 1], refs[-1]
    l_ref = None if rowsum_on_mxu else refs[-2]
    dk = q_ref.shape[1] // n_comp
    vw = o_ref.shape[1]
    m_ref[...] = jnp.full(m_ref.shape, -jnp.inf, F32)
    if l_ref is not None:
        l_ref[...] = jnp.zeros(l_ref.shape, F32)
    acc_ref[...] = jnp.zeros(acc_ref.shape, F32)

    def chunk_scores(k, c):
        return _scores(q_ref[:, c * dk:(c + 1) * dk], k[:, c * dk:(c + 1) * dk])

    def put_scores(slot, j):
        k = kl_ref[pl.ds(pl.multiple_of(j * tk, tk), tk), :]
        for c in range(n_comp):
            s_ref[slot, c] = chunk_scores(k, c)

    def consume(slot, j):
        v = vl_ref[pl.ds(pl.multiple_of(j * tk, tk), tk), :]
        for c in range(n_comp):
            _online_update(s_ref[slot, c], v, m_ref, l_ref, acc_ref, c)

    kc = kc_ref[...]
    s_ctx = [chunk_scores(kc, c) for c in range(n_comp)]
    if n_lat:
        put_scores(0, 0)
    for c in range(n_comp):
        _online_update(s_ctx[c], vc_ref[...], m_ref, l_ref, acc_ref, c)
    if n_lat:
        def pair(i, last):
            put_scores(1, 2 * i + 1)
            consume(0, 2 * i)
            if not last:
                put_scores(0, 2 * i + 2)
            consume(1, 2 * i + 1)

        def body(i, carry):
            pair(i, False)
            return carry

        lax.fori_loop(0, n_lat // 2 - 1, body, 0)
        pair(n_lat // 2 - 1, True)
    if rowsum_on_mxu:
        outs = [acc_ref[c][:, :vw] / acc_ref[c][:, vw:] for c in range(n_comp)]
    else:
        outs = [acc_ref[c] / jnp.sum(l_ref[c], axis=1, keepdims=True) for c in range(n_comp)]
    o_ref[...] = finalize(outs, *extras).astype(o_ref.dtype)


def _mla_finalize(outs):
    return outs[0]


def _diff_finalize(outs, lam_ref, g_ref, *, lam_init):
    lv = lam_ref[...]
    lam = (jnp.exp(jnp.sum(lv[0:1, :] * lv[1:2, :], axis=1, keepdims=True))
           - jnp.exp(jnp.sum(lv[2:3, :] * lv[3:4, :], axis=1, keepdims=True)) + lam_init)
    o = outs[0] - lam * outs[1]
    return o * lax.rsqrt(jnp.mean(o * o, axis=-1, keepdims=True) + EPS) * g_ref[...] * (1.0 - lam_init)


def _attention(finalize, n_comp, q, k, v, extra, heads, qw, vw, v_col0, batch, seq, ctx_len, latent, tq, tk):
    lat_ctx_blocks = batch * seq // ctx_len
    ctx_spec = lambda width, c0: pl.BlockSpec((ctx_len, width), lambda b, h, i: (lat_ctx_blocks + b, c0 + h))
    if latent:
        q_spec = pl.BlockSpec((tq, qw), lambda b, h, i: (b * (seq // tq) + i, h))
        in_specs = [q_spec, ctx_spec(qw, 0), ctx_spec(vw, v_col0),
                    pl.BlockSpec((seq, qw), lambda b, h, i: (b, h)),
                    pl.BlockSpec((seq, vw), lambda b, h, i: (b, v_col0 + h))]
        args = [q, k, v, k, v]
        grid = (batch, heads, seq // tq)
        out_spec = pl.BlockSpec((tq, vw), lambda b, h, i: (b * (seq // tq) + i, h))
        out_rows = batch * seq
        n_lat = seq // tk
        assert n_lat >= 2 and n_lat % 2 == 0
    else:
        tq = tk = ctx_len
        in_specs = [ctx_spec(qw, 0), ctx_spec(qw, 0), ctx_spec(vw, v_col0)]
        args = [q, k, v]
        grid = (batch, heads, 1)
        out_spec = pl.BlockSpec((ctx_len, vw), lambda b, h, i: (b, h))
        out_rows = batch * ctx_len
        n_lat = 0
    for e in extra:
        in_specs.append(pl.BlockSpec(e.shape, lambda b, h, i: (0, 0)))
        args.append(e)
    rowsum_on_mxu = vw == LANES
    scratch = [pltpu.VMEM((2, n_comp, tq, tk), F32),
               pltpu.VMEM((n_comp, tq, LANES), F32)]
    if rowsum_on_mxu:
        scratch.append(pltpu.VMEM((n_comp, tq, vw + LANES), F32))
    else:
        scratch += [pltpu.VMEM((n_comp, tq, LANES), F32),
                    pltpu.VMEM((n_comp, tq, vw), F32)]
    return pl.pallas_call(
        functools.partial(_attn_kernel, n_comp=n_comp, n_lat=n_lat, tk=tk, rowsum_on_mxu=rowsum_on_mxu,
                          finalize=finalize),
        grid=grid,
        in_specs=in_specs,
        out_specs=out_spec,
        out_shape=jax.ShapeDtypeStruct((out_rows, heads * vw), BF16),
        scratch_shapes=scratch,
        compiler_params=_cparams(("parallel", "parallel", "arbitrary")),
        name="attn_latent" if latent else "attn_ctx",
    )(*args)


def _rope_tables(rows, rot_dim):
    axis_dim = rot_dim // 2
    inv = ROPE_THETA ** (-jnp.arange(0, axis_dim, 2, dtype=F32) / axis_dim)
    row = jnp.repeat(jnp.arange(rows, dtype=F32), GRID_W)
    colv = (jnp.arange(rows * GRID_W) % GRID_W).astype(F32)
    ar, ac = row[:, None] * inv, colv[:, None] * inv
    cos = jnp.concatenate([jnp.cos(ar), jnp.cos(ar), jnp.cos(ac), jnp.cos(ac)], axis=1)
    sin = jnp.concatenate([-jnp.sin(ar), jnp.sin(ar), -jnp.sin(ac), jnp.sin(ac)], axis=1)
    return cos, sin


def _token_tables(cos, sin, batch, ctx_rows, left, right):
    n = cos.shape[0]
    cos = jnp.concatenate([jnp.ones((n, left), F32), cos, jnp.ones((n, right), F32)], axis=1)
    sin = jnp.concatenate([jnp.zeros((n, left), F32), sin, jnp.zeros((n, right), F32)], axis=1)
    w = cos.shape[1]
    cos = jnp.concatenate([jnp.tile(cos, (batch, 1)), jnp.ones((ctx_rows, w), F32)], axis=0)
    sin = jnp.concatenate([jnp.tile(sin, (batch, 1)), jnp.zeros((ctx_rows, w), F32)], axis=0)
    return cos, sin


def _layout_w_in(w):
    d = w.shape[0]
    w = w.astype(BF16)
    s1, s2 = SSD_COLS, SSD_COLS + MLA_COLS
    ssd = w[:, :SSD_INNER + SSD_CONV_CH]
    dt = w[:, SSD_INNER + SSD_CONV_CH:s1]
    mla = w[:, s1:s2]
    diff = w[:, s2:]
    zeros = lambda n: jnp.zeros((d, n), w.dtype)
    main = jnp.concatenate([ssd, diff, mla, zeros(LANES - MLA_ROPE), zeros(LANES)], axis=1)
    dtw = jnp.concatenate([dt, zeros(LANES - 2 * SSD_HEADS)], axis=1)
    return main, dtw


def _layout_mla(w_uq, w_ukv, g_q, g_k):
    pad = MLA_QK_PAD - MLA_QK
    uq = jnp.pad(w_uq.reshape(MLA_Q_LORA, MLA_HEADS, MLA_QK), ((0, 0), (0, 0), (0, pad)))
    ukv = w_ukv.reshape(MLA_KV_LORA, MLA_HEADS, MLA_NOPE + MLA_V)
    return dict(
        w_uq=uq.reshape(MLA_Q_LORA, MLA_HEADS * MLA_QK_PAD).astype(BF16),
        w_ukn=ukv[:, :, :MLA_NOPE].reshape(MLA_KV_LORA, MLA_HEADS * MLA_NOPE).astype(BF16),
        w_uv=ukv[:, :, MLA_NOPE:].reshape(MLA_KV_LORA, MLA_HEADS * MLA_V).astype(BF16),
        g_q=jnp.pad(g_q, (0, pad)).reshape(1, MLA_QK_PAD),
        g_k=jnp.pad(g_k, (0, pad)).reshape(1, MLA_QK_PAD),
    )


def kernel(x, c, ctx, c_ctx, w_mod, b_mod, g_norm, w_ffn_gate, w_ffn_up, w_ffn_down, w_in, ssd_conv_w, ssd_conv_b, ssd_dt_bias, ssd_a_log, ssd_d, ssd_norm_g, mla_g_cq, mla_w_uq, mla_g_ckv, mla_w_ukv, mla_g_q, mla_g_k, diff_g_q, diff_g_k, diff_lambda, diff_subln_g, w_out):
    batch, seq, d = x.shape
    ctx_len = ctx.shape[1]
    depth = w_mod.shape[0]
    d_ff = w_ffn_gate.shape[-1]
    lat_rows, ctx_rows = batch * seq, batch * ctx_len
    n_rows = lat_rows + ctx_rows
    assert w_in.shape[-1] == IN_COLS and w_out.shape[1] == SSD_INNER + MLA_HEADS * MLA_V + DIFF_HEADS * DIFF_V
    assert seq % GRID_W == 0 and batch < MOD_ROWS
    tm = _pick(math.gcd(seq, ctx_rows), (512, 256, 128))
    tr = _pick(math.gcd(seq, ctx_len), (256, 128))
    assert seq % tm == 0 and ctx_rows % tm == 0 and seq % tr == 0 and ctx_len % tr == 0
    assert ctx_len % SSD_CHUNK == 0 and seq % SSD_CHUNK == 0 and lat_rows % ctx_len == 0
    tq = _pick(seq, (512, 256, 128))
    tk = _pick(seq // 2, (2048, 1024, 512, 256, 128))

    def group_of_row(r):
        return jnp.minimum(r // seq, batch)

    xs = (x.reshape(lat_rows, d), ctx.reshape(ctx_rows, d))
    c_rows = jnp.concatenate([c, c_ctx[None, :], jnp.zeros((MOD_ROWS - batch - 1, d), F32)], axis=0)
    mod = _modulation(c_rows, w_mod, b_mod).reshape(depth, MOD_ROWS, 1, N_MOD * d)

    rows = seq // GRID_W
    cos_m, sin_m = _token_tables(*_rope_tables(rows, MLA_ROPE), batch, ctx_rows, 0, LANES - MLA_ROPE)
    cos_d, sin_d = _token_tables(*_rope_tables(rows, DIFF_QK), batch, ctx_rows, 0, 0)

    tn_ff = _pick(d_ff, (1024, 512, 256, 128))
    tn_out = _pick(d, (1024, 512, 256, 128))
    tn_in = _pick(PROJ_COLS, (768,))

    w_gate, w_up, w_down = w_ffn_gate.astype(BF16), w_ffn_up.astype(BF16), w_ffn_down.astype(BF16)

    def ffn(xs, li, j, norm_idx, gate_idx, m_rows):
        h = _norm_mod(xs, g_norm[li, norm_idx], mod[li], norm_idx, m_rows, group_of_row, tr)
        act = _matmul_swiglu(h, w_gate, w_up, (li, j), m_rows, tm, tn_ff)
        return _matmul_resid([(act, w_down, (li, j))], xs, mod[li], gate_idx, 0.5, m_rows, group_of_row, tm, tn_out,
                             single_buffer=True)

    for li in range(depth):
        need_ctx = li < depth - 1
        out_rows = n_rows if need_ctx else lat_rows
        lam_init = 0.8 - 0.6 * math.exp(-0.3 * li)

        xs = (ffn(xs, li, 0, 0, 2, n_rows),)

        h = _norm_mod(xs, g_norm[li, 1], mod[li], 1, n_rows, group_of_row, tr)
        w_main, w_dt = _layout_w_in(w_in[li])
        proj = _matmul(h, w_main, n_rows, BF16, tm, tn_in)
        dt_raw = _matmul(h, w_dt, n_rows, F32, tm, LANES)

        xbc = _ssd_conv(proj, ssd_conv_w[li], ssd_conv_b[li], n_rows, lat_rows, seq, ctx_len, tr)
        yf, yb = _ssd_scan(xbc, dt_raw, ssd_dt_bias[li], ssd_a_log[li], batch, seq, ctx_len)
        ssd_o = _ssd_out(yf, yb, xbc, proj, ssd_d[li], ssd_norm_g[li], out_rows, tr)

        mla_p = _layout_mla(mla_w_uq[li], mla_w_ukv[li], mla_g_q[li], mla_g_k[li])
        mla_p['g_cq'] = mla_g_cq[li].reshape(1, MLA_Q_LORA)
        mla_p['g_ckv'] = mla_g_ckv[li].reshape(1, MLA_KV_LORA)
        mq, mk, mv = _mla_prep(proj, mla_p, cos_m, sin_m, n_rows, tr)
        mla_args = (_mla_finalize, 1, mq, mk, mv, [], MLA_HEADS, MLA_QK_PAD, MLA_V, 0, batch, seq, ctx_len)
        mla_o = _attention(*mla_args, True, tq, tk)

        dq, dk = _diff_prep(proj, diff_g_q[li], diff_g_k[li], cos_d, sin_d, n_rows, tr)
        diff_fin = functools.partial(_diff_finalize, lam_init=lam_init)
        extra = [diff_lambda[li], diff_subln_g[li].reshape(1, DIFF_V)]
        diff_args = (diff_fin, 2, dq, dk, proj, extra, DIFF_HEADS, 2 * DIFF_QK, DIFF_V, PC_DV // DIFF_V,
                     batch, seq, ctx_len)
        diff_o = _attention(*diff_args, True, tq, tk)
        if need_ctx:
            mla_o = jnp.concatenate([mla_o, _attention(*mla_args, False, tq, tk)], axis=0)
            diff_o = jnp.concatenate([diff_o, _attention(*diff_args, False, tq, tk)], axis=0)

        wo = w_out[li].astype(BF16)
        o1, o2 = SSD_INNER, SSD_INNER + MLA_HEADS * MLA_V
        xs = _matmul_resid([(ssd_o, wo[:o1], ()), (mla_o, wo[o1:o2], ()), (diff_o, wo[o2:], ())], xs, mod[li], 5,
                           1.0, out_rows, group_of_row, tm, tn_out)

        xs = (ffn((xs,), li, 1, 2, 8, out_rows),)

    return xs[0][:lat_rows].reshape(batch, seq, d)
```
